```python
import jax, jax.numpy as jnp
from jax import lax
import numpy as np

D_MODEL = 1024
BATCH = 8
SEQ = 8192
DEPTH = 1
DEC_BATCH = 128
DEC_SEQ = 8
PAST_LEN = 8192
PAGE_SIZE = 128

N_HEADS = 8
HEAD_DIM = 64
D_ATTN = N_HEADS * HEAD_DIM
D_CONV = D_MODEL // 2
CONV_WIDTH = 3
Q_BLOCK = 128
RMS_EPS = 1e-6
NEG_INF = -1e30
SPLIT_SIZES = (D_CONV,) * 4 + (D_ATTN,) * 4 + (N_HEADS,) + (D_MODEL,) * 2
SPLIT_POINTS = tuple(int(p) for p in np.cumsum(SPLIT_SIZES)[:-1])
D_IN = int(sum(SPLIT_SIZES))

kernel_name = "hybrid_shortconv_fox_parallel_decode"


def rms_norm(x, g):
    xf = x.astype(jnp.float32)
    y = xf * lax.rsqrt(jnp.mean(xf * xf, axis=-1, keepdims=True) + RMS_EPS)
    return (y * g.astype(jnp.float32)).astype(x.dtype)


def mixer_inputs(x, norm_g, w_in, q_g, k_g, f_bias):
    b, t = x.shape[:2]
    h = rms_norm(x, norm_g)
    parts = jnp.split(h @ w_in, SPLIT_POINTS, axis=-1)
    cb, cc, cx, cz, q, k, v, az, fl, gc, ga = parts
    q = rms_norm(q.reshape(b, t, N_HEADS, HEAD_DIM), q_g)
    k = rms_norm(k.reshape(b, t, N_HEADS, HEAD_DIM), k_g)
    v = v.reshape(b, t, N_HEADS, HEAD_DIM)
    logf = jax.nn.log_sigmoid(fl.astype(jnp.float32) + f_bias.astype(jnp.float32)).astype(x.dtype)
    u = cc * cx
    return u, cb, cz, q, k, v, logf, az, gc, ga


def causal_conv(u, state, w):
    t = u.shape[1]
    ue = jnp.concatenate([state.astype(u.dtype), u], axis=1)
    y = sum(w[i] * ue[:, i:i + t] for i in range(CONV_WIDTH))
    return y, ue[:, -(CONV_WIDTH - 1):]


def attend(q, k, v, cq, ck, qpos, kpos):
    s = jnp.einsum('bqhd,bkhd->bhqk', q, k).astype(jnp.float32) * (HEAD_DIM ** -0.5)
    s = s + (jnp.swapaxes(cq, 1, 2)[:, :, :, None] - jnp.swapaxes(ck, 1, 2)[:, :, None, :])
    mask = kpos[None, :] <= qpos[:, None]
    s = jnp.where(mask, s, NEG_INF)
    p = jax.nn.softmax(s, axis=-1).astype(v.dtype)
    return jnp.einsum('bhqk,bkhd->bqhd', p, v)


def prompt_attention(q, k, v, logf):
    b, s = q.shape[:2]
    nb = s // Q_BLOCK
    c = jnp.cumsum(logf.astype(jnp.float32), axis=1)
    qb = q.reshape(b, nb, Q_BLOCK, N_HEADS, HEAD_DIM).swapaxes(0, 1)
    cqb = c.reshape(b, nb, Q_BLOCK, N_HEADS).swapaxes(0, 1)
    starts = jnp.arange(nb, dtype=jnp.int32) * Q_BLOCK
    kpos = jnp.arange(s, dtype=jnp.int32)

    def block(args):
        qi, ci, st = args
        return attend(qi, k, v, ci, c, st + jnp.arange(Q_BLOCK, dtype=jnp.int32), kpos)

    o = lax.map(block, (qb, cqb, starts))
    return o.swapaxes(0, 1).reshape(b, s, N_HEADS, HEAD_DIM)


def sample_attention(q, k, v, logf, kpool, vpool, lpool, page_table):
    past = page_table.shape[1] * PAGE_SIZE
    t = q.shape[1]
    qpos = past + jnp.arange(t, dtype=jnp.int32)
    kpos = jnp.arange(past + t, dtype=jnp.int32)

    def one(args):
        pt, qi, ki, vi, li = args
        kp = kpool[pt].reshape(past, N_HEADS, HEAD_DIM)
        vp = vpool[pt].reshape(past, N_HEADS, HEAD_DIM)
        lp = lpool[pt].reshape(past, N_HEADS)
        ka = jnp.concatenate([kp, ki.astype(kp.dtype)], axis=0)
        va = jnp.concatenate([vp, vi.astype(vp.dtype)], axis=0)
        c = jnp.cumsum(jnp.concatenate([lp.astype(jnp.float32), li.astype(jnp.float32)], axis=0), axis=0)
        return attend(qi[None], ka[None], va[None], c[None, past:], c[None], qpos, kpos)[0]

    return lax.map(one, (page_table, q, k, v, logf))


def mixer_outputs(x, conv_y, cb, cz, attn_o, az, gc, ga, w_conv_out, w_attn_out, w_o):
    b, t = x.shape[:2]
    yc = (cb * conv_y * jax.nn.silu(cz)) @ w_conv_out
    ya = (attn_o.reshape(b, t, D_ATTN) * jax.nn.silu(az)) @ w_attn_out
    merged = jax.nn.sigmoid(gc) * yc + jax.nn.sigmoid(ga) * ya
    return x + merged @ w_o


def setup_inputs(seed: int = 0) -> dict:
    key = jax.random.key(seed)
    ks = jax.random.split(key, 20)
    n_pages = PAST_LEN // PAGE_SIZE
    n_used = DEC_BATCH * n_pages
    n_phys = (n_used * 5) // 4
    f32 = jnp.float32
    x_prompt = jax.random.normal(ks[0], (BATCH, SEQ, D_MODEL), f32)
    x_sample = jax.random.normal(ks[1], (DEC_BATCH, DEC_SEQ, D_MODEL), f32)
    cache_k = jax.random.normal(ks[2], (DEPTH, n_phys, PAGE_SIZE, N_HEADS, HEAD_DIM), f32)
    cache_v = jax.random.normal(ks[3], (DEPTH, n_phys, PAGE_SIZE, N_HEADS, HEAD_DIM), f32)
    cache_logf = jax.nn.log_sigmoid(3.0 + jax.random.normal(ks[4], (DEPTH, n_phys, PAGE_SIZE, N_HEADS), f32))
    state_conv = jax.random.normal(ks[5], (DEPTH, DEC_BATCH, CONV_WIDTH - 1, D_CONV), f32)
    page_table = jax.random.permutation(ks[6], n_phys)[:n_used].reshape(DEC_BATCH, n_pages).astype(jnp.int32)
    norm_g = 1.0 + 0.01 * jax.random.normal(ks[7], (DEPTH, D_MODEL), f32)
    w_in = jax.random.normal(ks[8], (DEPTH, D_MODEL, D_IN), f32) * D_MODEL ** -0.5
    conv_w = jax.random.normal(ks[9], (DEPTH, CONV_WIDTH, D_CONV), f32) * CONV_WIDTH ** -0.5
    q_norm_g = 1.0 + 0.01 * jax.random.normal(ks[10], (DEPTH, HEAD_DIM), f32)
    k_norm_g = 1.0 + 0.01 * jax.random.normal(ks[11], (DEPTH, HEAD_DIM), f32)
    f_bias = jnp.linspace(1.0, 6.0, N_HEADS, dtype=f32)[None] + 0.01 * jax.random.normal(ks[12], (DEPTH, N_HEADS), f32)
    w_conv_out = jax.random.normal(ks[13], (DEPTH, D_CONV, D_MODEL), f32) * D_CONV ** -0.5
    w_attn_out = jax.random.normal(ks[14], (DEPTH, D_ATTN, D_MODEL), f32) * D_ATTN ** -0.5
    w_o = jax.random.normal(ks[15], (DEPTH, D_MODEL, D_MODEL), f32) * D_MODEL ** -0.5
    return {"x_prompt": x_prompt, "x_sample": x_sample, "cache_k": cache_k, "cache_v": cache_v,
            "cache_logf": cache_logf, "state_conv": state_conv, "page_table": page_table,
            "norm_g": norm_g, "w_in": w_in, "conv_w": conv_w, "q_norm_g": q_norm_g,
            "k_norm_g": k_norm_g, "f_bias": f_bias, "w_conv_out": w_conv_out,
            "w_attn_out": w_attn_out, "w_o": w_o}


def reference(x_prompt, x_sample, cache_k, cache_v, cache_logf, state_conv, page_table,
              norm_g, w_in, conv_w, q_norm_g, k_norm_g, f_bias, w_conv_out, w_attn_out, w_o):
    xp, xs = x_prompt, x_sample
    kp_l, vp_l, lp_l, cp_l = [], [], [], []
    ks_l, vs_l, ls_l, cs_l = [], [], [], []
    for l in range(DEPTH):
        u, cb, cz, q, k, v, logf, az, gc, ga = mixer_inputs(xp, norm_g[l], w_in[l], q_norm_g[l], k_norm_g[l], f_bias[l])
        zero_state = jnp.zeros((u.shape[0], CONV_WIDTH - 1, D_CONV), u.dtype)
        conv_y, conv_new = causal_conv(u, zero_state, conv_w[l])
        o = prompt_attention(q, k, v, logf)
        xp = mixer_outputs(xp, conv_y, cb, cz, o, az, gc, ga, w_conv_out[l], w_attn_out[l], w_o[l])
        kp_l.append(k); vp_l.append(v); lp_l.append(logf); cp_l.append(conv_new)
        u, cb, cz, q, k, v, logf, az, gc, ga = mixer_inputs(xs, norm_g[l], w_in[l], q_norm_g[l], k_norm_g[l], f_bias[l])
        conv_y, conv_new = causal_conv(u, state_conv[l], conv_w[l])
        o = sample_attention(q, k, v, logf, cache_k[l], cache_v[l], cache_logf[l], page_table)
        xs = mixer_outputs(xs, conv_y, cb, cz, o, az, gc, ga, w_conv_out[l], w_attn_out[l], w_o[l])
        ks_l.append(k); vs_l.append(v); ls_l.append(logf); cs_l.append(conv_new)
    return (xp, xs, jnp.stack(kp_l), jnp.stack(vp_l), jnp.stack(lp_l), jnp.stack(cp_l),
            jnp.stack(ks_l), jnp.stack(vs_l), jnp.stack(ls_l), jnp.stack(cs_l))
```

```python
import functools

import jax
import jax.numpy as jnp
import numpy as np
from jax import lax
from jax.experimental import pallas as pl
from jax.experimental.pallas import tpu as pltpu

N_HEADS = 8
HEAD_DIM = 64
D_ATTN = N_HEADS * HEAD_DIM
CONV_WIDTH = 3
RMS_EPS = 1e-6
NEG_INF = -1e30
LOG2E = 1.4426950408889634
LANES = 128
SUBLANES = 8
VMEM_LIMIT = 56 * 1024 * 1024
PAGES_PER_STEP = 8

f32 = jnp.float32
bf16 = jnp.bfloat16


def _dot(a, b):
    return jnp.dot(a, b, preferred_element_type=f32)


def _dot_nt(a, b):
    return lax.dot_general(a, b, (((1,), (1,)), ((), ())), preferred_element_type=f32)


def _split3(a):
    hi = a.astype(bf16)
    r = a - hi.astype(f32)
    mid = r.astype(bf16)
    lo = (r - mid.astype(f32)).astype(bf16)
    return hi, mid, lo


def _dot3_lhs(a, b01):
    hi, mid, lo = _split3(a)
    return _dot(hi, b01) + _dot(mid, b01) + _dot(lo, b01)


def _dot3_rhs(a01, b):
    hi, mid, lo = _split3(b)
    return _dot(a01, hi) + _dot(a01, mid) + _dot(a01, lo)


def _log_sigmoid(z):
    return jnp.minimum(z, 0.0) - jnp.log1p(jnp.exp(-jnp.abs(z)))


def _silu(z):
    return z * jax.nn.sigmoid(z)


def _const_spec(shape):
    nd = len(shape)
    return pl.BlockSpec(shape, lambda *_: (0,) * nd, pipeline_mode=pl.Buffered(1))


def _inproj_common(x, ng_ref, wc_ref, wf_ref, waz_ref, wgc_ref, wga_ref, fb_ref):
    ms = jnp.mean(x * x, axis=-1, keepdims=True)
    h = (x * lax.rsqrt(ms + RMS_EPS) * ng_ref[...]).astype(bf16)
    dc = wc_ref.shape[1] // 4
    cb = _dot(h, wc_ref[:, 0 * dc:1 * dc])
    cc = _dot(h, wc_ref[:, 1 * dc:2 * dc])
    cx = _dot(h, wc_ref[:, 2 * dc:3 * dc])
    cz = _dot(h, wc_ref[:, 3 * dc:4 * dc])
    u = cc * cx
    gate_c = cb * _silu(cz)
    z = _dot(h, wf_ref[...]) + fb_ref[...]
    lane = lax.broadcasted_iota(jnp.int32, z.shape, 1)
    lf = jnp.where(lane < N_HEADS, _log_sigmoid(z), 0.0)
    sz = _silu(_dot(h, waz_ref[...])).astype(bf16)
    sgc = jax.nn.sigmoid(_dot(h, wgc_ref[...]))
    sga = jax.nn.sigmoid(_dot(h, wga_ref[...])).astype(bf16)
    return h, u, gate_c, lf, sz, sgc, sga


def _head_norm_T(xT, g_col):
    n = xT.shape[1]
    x3 = xT.reshape(N_HEADS, HEAD_DIM, n)
    ms = jnp.mean(x3 * x3, axis=1, keepdims=True)
    return (x3 * lax.rsqrt(ms + RMS_EPS)).reshape(D_ATTN, n) * g_col


def _inproj_prompt_kernel(x_ref, ng_ref, wc_ref, wf_ref, waz_ref, wgc_ref, wga_ref, fb_ref, cw_ref, wco_ref,
                          wqT_ref, wkT_ref, wvT_ref, wfT_ref, qg_ref, kg_ref, fbc_ref, tri_ref,
                          kT_ref, kb_ref, vT_ref, vTb_ref, qT_ref, ek_ref, lfT_ref, sz_ref, sga_ref, mc_ref, cn_ref,
                          ucar_ref, ccar_ref):
    @pl.when(pl.program_id(1) == 0)
    def _():
        ucar_ref[...] = jnp.zeros_like(ucar_ref)
        ccar_ref[...] = jnp.zeros_like(ccar_ref)

    x = x_ref[0]
    tm = x.shape[0]
    h, u, gate_c, lf, sz, sgc, sga = _inproj_common(x, ng_ref, wc_ref, wf_ref, waz_ref, wgc_ref, wga_ref, fb_ref)

    rows = lax.broadcasted_iota(jnp.int32, u.shape, 0)
    prev1 = ucar_ref[SUBLANES - 1:SUBLANES, :]
    prev2 = ucar_ref[SUBLANES - 2:SUBLANES - 1, :]
    u1 = jnp.where(rows == 0, prev1, pltpu.roll(u, 1, 0))
    u2 = jnp.where(rows == 0, prev2, jnp.where(rows == 1, prev1, pltpu.roll(u, 2, 0)))
    cy = cw_ref[0:1, :] * u2 + cw_ref[1:2, :] * u1 + cw_ref[2:3, :] * u
    ucar_ref[...] = u[tm - SUBLANES:tm, :]
    cn_ref[0] = ucar_ref[SUBLANES - 2:SUBLANES, :]
    yc = _dot((gate_c * cy).astype(bf16), wco_ref[...])
    mc_ref[0] = (sgc * yc).astype(bf16)
    sga_ref[0] = sga
    sz_ref[0] = sz

    knT = _head_norm_T(_dot_nt(wkT_ref[...], h), kg_ref[...])
    kT_ref[0] = knT
    kb_ref[0] = knT.T.astype(bf16)
    vT = _dot_nt(wvT_ref[...], h)
    vT_ref[0] = vT
    vTb_ref[0] = vT.astype(bf16)
    qT_ref[0] = _head_norm_T(_dot_nt(wqT_ref[...], h), qg_ref[...]).astype(bf16)
    zT = _dot_nt(wfT_ref[...], h) + fbc_ref[...]
    lfT_ref[0] = _log_sigmoid(zT)[0:N_HEADS, :]

    c = ccar_ref[SUBLANES - 1:SUBLANES, :] + _dot3_rhs(tri_ref[...], lf)
    ccar_ref[...] = c[tm - SUBLANES:tm, :]
    hi, mid, lo = _split3(c * (-LOG2E))
    ek = hi.astype(f32) + pltpu.roll(mid.astype(f32), N_HEADS, 1) + pltpu.roll(lo.astype(f32), 2 * N_HEADS, 1)
    ek_ref[0] = ek.astype(bf16)


def _head_norm(xf, g_row, bd):
    ssq = _dot((xf * xf).astype(bf16), bd)
    return xf * lax.rsqrt(ssq * (1.0 / HEAD_DIM) + RMS_EPS) * g_row


def _inproj_sample_kernel(x_ref, ng_ref, wc_ref, wf_ref, waz_ref, wgc_ref, wga_ref, fb_ref, cw_ref, wco_ref,
                          wq_ref, wk_ref, wv_ref, qg_ref, kg_ref, bd_ref, s1_ref, s2_ref,
                          k_ref, v_ref, q_ref, lf_ref, sz_ref, sga_ref, mc_ref, u_ref, *, dec_seq):
    x = x_ref[...]
    h, u, gate_c, lf, sz, sgc, sga = _inproj_common(x, ng_ref, wc_ref, wf_ref, waz_ref, wgc_ref, wga_ref, fb_ref)
    t = lax.broadcasted_iota(jnp.int32, u.shape, 0) % dec_seq
    u1 = jnp.where(t == 0, s1_ref[...], pltpu.roll(u, 1, 0))
    u2 = jnp.where(t < 2, s2_ref[...], pltpu.roll(u, 2, 0))
    cy = cw_ref[0:1, :] * u2 + cw_ref[1:2, :] * u1 + cw_ref[2:3, :] * u
    u_ref[...] = u
    yc = _dot((gate_c * cy).astype(bf16), wco_ref[...])
    mc_ref[...] = (sgc * yc).astype(bf16)
    sga_ref[...] = sga
    sz_ref[...] = sz.astype(f32)
    k_ref[...] = _head_norm(_dot(h, wk_ref[...]), kg_ref[...], bd_ref[...])
    v_ref[...] = _dot(h, wv_ref[...])
    q_ref[...] = _head_norm(_dot(h, wq_ref[...]), qg_ref[...], bd_ref[...])
    lf_ref[...] = lf[:, :N_HEADS]


def _attn_prompt_kernel(qT_ref, kb_ref, ek_ref, vT_ref, sz_ref, g_ref, qw_ref, m_ref, l_ref, acc_ref, *, tile):
    hp = pl.program_id(1)
    qi = pl.program_id(2)
    qT = qT_ref[0]
    row = lax.broadcasted_iota(jnp.int32, qT.shape, 0)
    for j in range(2):
        top = jnp.where((row // HEAD_DIM) == j, qT, jnp.zeros_like(qT))
        head = 2 * hp + j
        sel = jnp.where((row < 3 * N_HEADS) & ((row % N_HEADS) == head), 1.0, 0.0).astype(bf16)
        qw_ref[j] = jnp.concatenate([top, sel], axis=0)
    m_ref[...] = jnp.full_like(m_ref, NEG_INF)
    l_ref[...] = jnp.zeros_like(l_ref)
    acc_ref[...] = jnp.zeros_like(acc_ref)

    def step(kt, masked):
        k0 = pl.multiple_of(kt * tile, tile)
        lhs = jnp.concatenate([kb_ref[0, pl.ds(k0, tile), :], ek_ref[0, pl.ds(k0, tile), :]], axis=1)
        for j in range(2):
            s = _dot(lhs, qw_ref[j])
            if masked:
                kidx = lax.broadcasted_iota(jnp.int32, s.shape, 0)
                qidx = lax.broadcasted_iota(jnp.int32, s.shape, 1)
                s = jnp.where(kidx <= qidx, s, NEG_INF)
            m_old = m_ref[j]
            m_new = jnp.maximum(m_old, jnp.max(s, axis=0, keepdims=True))
            alpha = jnp.exp2(m_old - m_new)
            p = jnp.exp2(s - m_new)
            l_ref[j] = alpha * l_ref[j] + jnp.sum(p, axis=0, keepdims=True)
            pv = _dot(vT_ref[0, j * HEAD_DIM:(j + 1) * HEAD_DIM, pl.ds(k0, tile)], p.astype(bf16))
            acc_ref[j] = alpha * acc_ref[j] + pv
            m_ref[j] = m_new

    def body(kt, carry):
        step(kt, False)
        return carry

    lax.fori_loop(0, qi, body, 0)
    step(qi, True)
    o = jnp.concatenate([acc_ref[0] / l_ref[0], acc_ref[1] / l_ref[1]], axis=0)
    g_ref[0] = (o.T * sz_ref[0].astype(f32)).astype(bf16)


def _logf_suffix_kernel(lp_ref, t_ref, o_ref):
    o_ref[...] = _dot3_lhs(lp_ref[...], t_ref[...])


def _attn_sample_kernel(pt_ref, q_ref, kn_ref, vn_ref, lfT_ref, sz_ref, ut_ref, *refs, dec_seq, n_pg):
    k_refs = refs[0:n_pg]
    v_refs = refs[n_pg:2 * n_pg]
    e_refs = refs[2 * n_pg:3 * n_pg]
    g_ref = refs[3 * n_pg]
    qbd_ref, kb_ref, vb_ref, run_ref, m_ref, l_ref, acc_ref = refs[3 * n_pg + 1:]
    c = pl.program_id(1)
    n_rows = dec_seq * N_HEADS

    def online_update(s, v_bf, v_feature_major):
        m_old = m_ref[...]
        m_new = jnp.maximum(m_old, jnp.max(s, axis=1, keepdims=True))
        alpha = jnp.exp2(m_old - m_new)
        p = jnp.exp2(s - m_new).astype(bf16)
        l_ref[...] = alpha * l_ref[...] + jnp.sum(p.astype(f32), axis=1, keepdims=True)
        pv = _dot_nt(p, v_bf) if v_feature_major else _dot(p, v_bf)
        acc_ref[...] = alpha * acc_ref[...] + pv
        m_ref[...] = m_new

    @pl.when(c == 0)
    def _():
        q = q_ref[...]
        q3 = jnp.broadcast_to(q[:, None, :], (dec_seq, N_HEADS, D_ATTN))
        hh = lax.broadcasted_iota(jnp.int32, q3.shape, 1)
        ln = lax.broadcasted_iota(jnp.int32, q3.shape, 2)
        qbd = jnp.where((ln // HEAD_DIM) == hh, q3, 0.0).reshape(n_rows, D_ATTN).astype(bf16)
        qbd_ref[...] = qbd
        m_ref[...] = jnp.full_like(m_ref, NEG_INF)
        l_ref[...] = jnp.zeros_like(l_ref)
        acc_ref[...] = jnp.zeros_like(acc_ref)
        run_ref[...] = jnp.zeros_like(run_ref)
        pad = jnp.zeros((LANES - dec_seq, D_ATTN), f32)
        kpad = jnp.concatenate([kn_ref[...], pad], axis=0).astype(bf16)
        vpad = jnp.concatenate([vn_ref[...], pad], axis=0).astype(bf16)
        s = _dot_nt(qbd, kpad)
        cnew = _dot3_lhs(lfT_ref[0], ut_ref[...])
        s3 = s.reshape(dec_seq, N_HEADS, LANES) - (cnew * LOG2E)[None]
        tt = lax.broadcasted_iota(jnp.int32, s3.shape, 0)
        jj = lax.broadcasted_iota(jnp.int32, s3.shape, 2)
        s3 = jnp.where(jj <= tt, s3, NEG_INF)
        online_update(s3.reshape(n_rows, LANES), vpad, False)

    bias = [None] * n_pg
    run = run_ref[...]
    for i in reversed(range(n_pg)):
        e = e_refs[i][0]
        bias[i] = (e[:, 0:LANES] + run) * LOG2E
        run = run + e[:, LANES:2 * LANES]
        kb_ref[:, i * LANES:(i + 1) * LANES] = k_refs[i][0].astype(bf16)
        vb_ref[:, i * LANES:(i + 1) * LANES] = v_refs[i][0].astype(bf16)
    run_ref[...] = run
    s = _dot(qbd_ref[...], kb_ref[...])
    b = jnp.concatenate(bias, axis=1)
    s3 = s.reshape(dec_seq, N_HEADS, n_pg * LANES) + b[None]
    online_update(s3.reshape(n_rows, n_pg * LANES), vb_ref[...], True)

    @pl.when(c == pl.num_programs(1) - 1)
    def _():
        o = acc_ref[...] / l_ref[...]
        o3 = o.reshape(dec_seq, N_HEADS, D_ATTN)
        hh = lax.broadcasted_iota(jnp.int32, o3.shape, 1)
        ln = lax.broadcasted_iota(jnp.int32, o3.shape, 2)
        o2 = jnp.sum(jnp.where((ln // HEAD_DIM) == hh, o3, 0.0), axis=1)
        g_ref[...] = o2 * sz_ref[...]


def _outproj_kernel(x_ref, g_ref, mc_ref, sga_ref, wao_ref, wo_ref, y_ref):
    ya = _dot(g_ref[...].astype(bf16), wao_ref[...])
    merged = mc_ref[...].astype(f32) + sga_ref[...].astype(f32) * ya
    y_ref[...] = x_ref[...] + _dot(merged.astype(bf16), wo_ref[...])


def _row_tile(n, pref):
    t = min(n, pref)
    while n % t:
        t //= 2
    return t


def _layer(xp, xs, cache_k, cache_v, cache_logf, state_conv, page_table,
           norm_g, w_in, conv_w, q_norm_g, k_norm_g, f_bias, w_conv_out, w_attn_out, w_o):
    B, S, D = xp.shape
    NS, T, _ = xs.shape
    n_phys = cache_k.shape[0]
    n_pages = page_table.shape[1]
    dc = conv_w.shape[1]
    tm = _row_tile(S, 512)
    tile = _row_tile(S, 512)
    n_pg = min(PAGES_PER_STEP, n_pages)
    cp = functools.partial(pltpu.CompilerParams, vmem_limit_bytes=VMEM_LIMIT)

    o0 = 4 * dc
    wb = w_in.astype(bf16)
    w_c = wb[:, 0:o0]
    w_q = wb[:, o0:o0 + D_ATTN]
    w_k = wb[:, o0 + D_ATTN:o0 + 2 * D_ATTN]
    w_v = wb[:, o0 + 2 * D_ATTN:o0 + 3 * D_ATTN]
    w_az = wb[:, o0 + 3 * D_ATTN:o0 + 4 * D_ATTN]
    o1 = o0 + 4 * D_ATTN
    w_f = jnp.pad(wb[:, o1:o1 + N_HEADS], ((0, 0), (0, LANES - N_HEADS)))
    w_gc = wb[:, o1 + N_HEADS:o1 + N_HEADS + D]
    w_ga = wb[:, o1 + N_HEADS + D:o1 + N_HEADS + 2 * D]
    ng = norm_g.reshape(1, D)
    kg = jnp.tile(k_norm_g, N_HEADS)
    qg = jnp.tile(q_norm_g, N_HEADS) * (HEAD_DIM ** -0.5 * LOG2E)
    fb = jnp.pad(f_bias, (0, LANES - N_HEADS))
    wco = w_conv_out.astype(bf16)
    wao = w_attn_out.astype(bf16)
    wo = w_o.astype(bf16)
    common_w = (ng, w_c, w_f, w_az, w_gc, w_ga, fb.reshape(1, LANES), conv_w, wco)
    common_specs = [_const_spec(a.shape) for a in common_w]

    tri = jnp.asarray(np.tril(np.ones((tm, tm), np.float32)), bf16)
    w_fT = w_f.T[0:2 * SUBLANES]
    prompt_w = (w_q.T, w_k.T, w_v.T, w_fT, qg.reshape(D_ATTN, 1), kg.reshape(D_ATTN, 1),
                fb[0:2 * SUBLANES].reshape(2 * SUBLANES, 1), tri)
    nt = S // tm
    row_spec = lambda w: pl.BlockSpec((1, tm, w), lambda b, j: (b, j, 0))
    colT_spec = lambda r: pl.BlockSpec((1, r, tm), lambda b, j: (b, 0, j))
    kT_p, kb_p, vT_p, vTb_p, qT_p, ek_p, lfT_p, sz_p, sga_p, mc_p, cn_p = pl.pallas_call(
        _inproj_prompt_kernel,
        grid=(B, nt),
        in_specs=[row_spec(D)] + common_specs + [_const_spec(a.shape) for a in prompt_w],
        out_specs=[colT_spec(D_ATTN), row_spec(D_ATTN), colT_spec(D_ATTN), colT_spec(D_ATTN), colT_spec(D_ATTN),
                   row_spec(LANES), colT_spec(N_HEADS), row_spec(D_ATTN), row_spec(D), row_spec(D),
                   pl.BlockSpec((1, CONV_WIDTH - 1, dc), lambda b, j: (b, 0, 0))],
        out_shape=[jax.ShapeDtypeStruct((B, D_ATTN, S), f32), jax.ShapeDtypeStruct((B, S, D_ATTN), bf16),
                   jax.ShapeDtypeStruct((B, D_ATTN, S), f32), jax.ShapeDtypeStruct((B, D_ATTN, S), bf16),
                   jax.ShapeDtypeStruct((B, D_ATTN, S), bf16), jax.ShapeDtypeStruct((B, S, LANES), bf16),
                   jax.ShapeDtypeStruct((B, N_HEADS, S), f32), jax.ShapeDtypeStruct((B, S, D_ATTN), bf16),
                   jax.ShapeDtypeStruct((B, S, D), bf16), jax.ShapeDtypeStruct((B, S, D), bf16),
                   jax.ShapeDtypeStruct((B, CONV_WIDTH - 1, dc), f32)],
        scratch_shapes=[pltpu.VMEM((SUBLANES, dc), f32), pltpu.VMEM((SUBLANES, LANES), f32)],
        compiler_params=cp(dimension_semantics=("arbitrary", "arbitrary")),
        name="inproj_prompt",
    )(xp, *common_w, *prompt_w)

    nq = S // tile
    pair = 2 * HEAD_DIM
    g_p = pl.pallas_call(
        functools.partial(_attn_prompt_kernel, tile=tile),
        grid=(B, N_HEADS // 2, nq),
        in_specs=[pl.BlockSpec((1, pair, tile), lambda b, hp, qi: (b, hp, qi)),
                  pl.BlockSpec((1, S, pair), lambda b, hp, qi: (b, 0, hp)),
                  pl.BlockSpec((1, S, LANES), lambda b, hp, qi: (b, 0, 0)),
                  pl.BlockSpec((1, pair, S), lambda b, hp, qi: (b, hp, 0)),
                  pl.BlockSpec((1, tile, pair), lambda b, hp, qi: (b, qi, hp))],
        out_specs=pl.BlockSpec((1, tile, pair), lambda b, hp, qi: (b, qi, hp)),
        out_shape=jax.ShapeDtypeStruct((B, S, D_ATTN), bf16),
        scratch_shapes=[pltpu.VMEM((2, 2 * LANES, tile), bf16), pltpu.VMEM((2, 1, tile), f32),
                        pltpu.VMEM((2, 1, tile), f32), pltpu.VMEM((2, HEAD_DIM, tile), f32)],
        compiler_params=cp(dimension_semantics=("arbitrary", "arbitrary", "arbitrary")),
        name="attn_prompt",
    )(qT_p, kb_p, ek_p, vTb_p, sz_p)

    def outproj(x2, g2, mc2, sga2, name):
        n = x2.shape[0]
        t = _row_tile(n, 512)
        rs = lambda w: pl.BlockSpec((t, w), lambda i: (i, 0))
        return pl.pallas_call(
            _outproj_kernel,
            grid=(n // t,),
            in_specs=[rs(D), rs(D_ATTN), rs(D), rs(D), _const_spec(wao.shape), _const_spec(wo.shape)],
            out_specs=rs(D),
            out_shape=jax.ShapeDtypeStruct((n, D), f32),
            compiler_params=cp(dimension_semantics=("arbitrary",)),
            name=name,
        )(x2, g2, mc2, sga2, wao, wo)

    y_p = outproj(xp.reshape(B * S, D), g_p.reshape(B * S, D_ATTN), mc_p.reshape(B * S, D),
                  sga_p.reshape(B * S, D), "outproj_prompt").reshape(B, S, D)

    R = NS * T
    ts = _row_tile(R, 512)
    zrow = jnp.zeros((NS, T - 1, dc), f32)
    s1 = jnp.concatenate([state_conv[:, 1:2], zrow], axis=1).reshape(R, dc)
    s2 = jnp.concatenate([state_conv, zrow[:, 1:]], axis=1).reshape(R, dc)
    head_of = np.arange(D_ATTN) // HEAD_DIM
    bd = jnp.asarray(head_of[:, None] == head_of[None, :], bf16)
    sample_w = (w_q, w_k, w_v, qg.reshape(1, D_ATTN), kg.reshape(1, D_ATTN), bd)
    rs = lambda w: pl.BlockSpec((ts, w), lambda i: (i, 0))
    k_s, v_s, q_s, lf_s, sz_s, sga_s, mc_s, u_s = pl.pallas_call(
        functools.partial(_inproj_sample_kernel, dec_seq=T),
        grid=(R // ts,),
        in_specs=[rs(D)] + common_specs + [_const_spec(a.shape) for a in sample_w] + [rs(dc), rs(dc)],
        out_specs=[rs(D_ATTN), rs(D_ATTN), rs(D_ATTN), rs(N_HEADS), rs(D_ATTN), rs(D), rs(D), rs(dc)],
        out_shape=[jax.ShapeDtypeStruct((R, D_ATTN), f32), jax.ShapeDtypeStruct((R, D_ATTN), f32),
                   jax.ShapeDtypeStruct((R, D_ATTN), f32), jax.ShapeDtypeStruct((R, N_HEADS), f32),
                   jax.ShapeDtypeStruct((R, D_ATTN), f32), jax.ShapeDtypeStruct((R, D), bf16),
                   jax.ShapeDtypeStruct((R, D), bf16), jax.ShapeDtypeStruct((R, dc), f32)],
        compiler_params=cp(dimension_semantics=("arbitrary",)),
        name="inproj_sample",
    )(xs.reshape(R, D), *common_w, *sample_w, s1, s2)

    page = cache_logf.shape[1]
    assert page == LANES
    pos = np.arange(page)
    tmat = jnp.asarray(np.concatenate([pos[:, None] > pos[None, :], np.ones((page, page), bool)], axis=1), bf16)
    lp2 = jnp.transpose(cache_logf, (0, 2, 1)).reshape(n_phys * N_HEADS, page)
    tp = _row_tile(n_phys * N_HEADS, 2048)
    esuf = pl.pallas_call(
        _logf_suffix_kernel,
        grid=(n_phys * N_HEADS // tp,),
        in_specs=[pl.BlockSpec((tp, page), lambda i: (i, 0)), _const_spec(tmat.shape)],
        out_specs=pl.BlockSpec((tp, 2 * page), lambda i: (i, 0)),
        out_shape=jax.ShapeDtypeStruct((n_phys * N_HEADS, 2 * page), f32),
        compiler_params=cp(dimension_semantics=("arbitrary",)),
        name="logf_suffix",
    )(lp2, tmat).reshape(n_phys, N_HEADS, 2 * page)

    n_chunks = n_pages // n_pg
    lfT = jnp.pad(jnp.swapaxes(lf_s.reshape(NS, T, N_HEADS), 1, 2), ((0, 0), (0, 0), (0, LANES - T)))
    ut = jnp.asarray(np.triu(np.ones((LANES, LANES), np.float32)), bf16)
    ck2 = jnp.transpose(cache_k, (0, 2, 3, 1)).reshape(n_phys, D_ATTN, page)
    cv2 = jnp.transpose(cache_v, (0, 2, 3, 1)).reshape(n_phys, D_ATTN, page)

    def page_spec(shape, i):
        def imap(b, c, pt):
            return (pt[b, (n_chunks - 1 - c) * n_pg + i], 0, 0)
        return pl.BlockSpec(shape, imap)

    seq_spec = pl.BlockSpec((T, D_ATTN), lambda b, c, pt: (b, 0))
    g_s = pl.pallas_call(
        functools.partial(_attn_sample_kernel, dec_seq=T, n_pg=n_pg),
        grid_spec=pltpu.PrefetchScalarGridSpec(
            num_scalar_prefetch=1,
            grid=(NS, n_chunks),
            in_specs=[seq_spec, seq_spec, seq_spec,
                      pl.BlockSpec((1, N_HEADS, LANES), lambda b, c, pt: (b, 0, 0)),
                      seq_spec,
                      pl.BlockSpec((LANES, LANES), lambda b, c, pt: (0, 0))]
                     + [page_spec((1, D_ATTN, page), i) for i in range(n_pg)]
                     + [page_spec((1, D_ATTN, page), i) for i in range(n_pg)]
                     + [page_spec((1, N_HEADS, 2 * page), i) for i in range(n_pg)],
            out_specs=seq_spec,
            scratch_shapes=[pltpu.VMEM((T * N_HEADS, D_ATTN), bf16),
                            pltpu.VMEM((D_ATTN, n_pg * page), bf16), pltpu.VMEM((D_ATTN, n_pg * page), bf16),
                            pltpu.VMEM((N_HEADS, LANES), f32), pltpu.VMEM((T * N_HEADS, 1), f32),
                            pltpu.VMEM((T * N_HEADS, 1), f32), pltpu.VMEM((T * N_HEADS, D_ATTN), f32)]),
        out_shape=jax.ShapeDtypeStruct((R, D_ATTN), f32),
        compiler_params=cp(dimension_semantics=("arbitrary", "arbitrary")),
        name="attn_sample",
    )(page_table, q_s, k_s, v_s, lfT, sz_s, ut, *([ck2] * n_pg), *([cv2] * n_pg), *([esuf] * n_pg))

    y_s = outproj(xs.reshape(R, D), g_s, mc_s, sga_s, "outproj_sample").reshape(NS, T, D)

    unT = lambda a: jnp.transpose(a.reshape(B, N_HEADS, HEAD_DIM, S), (0, 3, 1, 2))
    return (y_p, y_s, unT(kT_p), unT(vT_p), jnp.transpose(lfT_p, (0, 2, 1)), cn_p,
            k_s.reshape(NS, T, N_HEADS, HEAD_DIM), v_s.reshape(NS, T, N_HEADS, HEAD_DIM),
            lf_s.reshape(NS, T, N_HEADS), u_s.reshape(NS, T, dc)[:, T - (CONV_WIDTH - 1):])


def kernel(x_prompt, x_sample, cache_k, cache_v, cache_logf, state_conv, page_table, norm_g, w_in, conv_w,
           q_norm_g, k_norm_g, f_bias, w_conv_out, w_attn_out, w_o):
    depth = norm_g.shape[0]
    xp, xs = x_prompt, x_sample
    outs = []
    for l in range(depth):
        res = _layer(xp, xs, cache_k[l], cache_v[l], cache_logf[l], state_conv[l], page_table,
                     norm_g[l], w_in[l], conv_w[l], q_norm_g[l], k_norm_g[l], f_bias[l],
                     w_conv_out[l], w_attn_out[l], w_o[l])
        xp, xs = res[0], res[1]
        outs.append(res[2:])
    stacked = tuple(jnp.stack([o[i] for o in outs]) for i in range(8))
    return (xp, xs) + stacked
```

```python
import functools

import jax
import jax.numpy as jnp
import numpy as np
from jax import lax
from jax.experimental import pallas as pl
from jax.experimental.pallas import tpu as pltpu

N_HEADS = 8
HEAD_DIM = 64
D_ATTN = N_HEADS * HEAD_DIM
CONV_WIDTH = 3
RMS_EPS = 1e-6
NEG_INF = -1e30
LOG2E = 1.4426950408889634
LANES = 128
SUBLANES = 8
VMEM_LIMIT = 56 * 1024 * 1024
PAGES_PER_STEP = 16
SHIFT_MARGIN = 1.02
MAX_FIXED_SHIFT_BOUND = 40.0

f32 = jnp.float32
bf16 = jnp.bfloat16


def _dot(a, b):
    return jnp.dot(a, b, preferred_element_type=f32)


def _dot_nt(a, b):
    return lax.dot_general(a, b, (((1,), (1,)), ((), ())), preferred_element_type=f32)


def _split3(a):
    hi = a.astype(bf16)
    r = a - hi.astype(f32)
    mid = r.astype(bf16)
    lo = (r - mid.astype(f32)).astype(bf16)
    return hi, mid, lo


def _dot3_lhs(a, b01):
    hi, mid, lo = _split3(a)
    return _dot(hi, b01) + _dot(mid, b01) + _dot(lo, b01)


def _log_sigmoid(z):
    return jnp.minimum(z, 0.0) - jnp.log1p(jnp.exp(-jnp.abs(z)))


def _silu(z):
    return z * jax.nn.sigmoid(z)


def _const_spec(shape):
    nd = len(shape)
    return pl.BlockSpec(shape, lambda *_: (0,) * nd, pipeline_mode=pl.Buffered(1))


def _inproj_common(x, ng_ref, wc_ref, waz_ref, wgc_ref, wga_ref):
    ms = jnp.mean(x * x, axis=-1, keepdims=True)
    h = (x * lax.rsqrt(ms + RMS_EPS) * ng_ref[...]).astype(bf16)
    dc = wc_ref.shape[1] // 4
    cb = _dot(h, wc_ref[:, 0 * dc:1 * dc])
    cc = _dot(h, wc_ref[:, 1 * dc:2 * dc])
    cx = _dot(h, wc_ref[:, 2 * dc:3 * dc])
    cz = _dot(h, wc_ref[:, 3 * dc:4 * dc])
    u = cc * cx
    gate_c = cb * _silu(cz)
    sz = _silu(_dot(h, waz_ref[...])).astype(bf16)
    sgc = jax.nn.sigmoid(_dot(h, wgc_ref[...]))
    sga = jax.nn.sigmoid(_dot(h, wga_ref[...])).astype(bf16)
    return h, u, gate_c, sz, sgc, sga


def _head_norm_T(xT, g_col):
    n = xT.shape[1]
    x3 = xT.reshape(N_HEADS, HEAD_DIM, n)
    ms = jnp.mean(x3 * x3, axis=1, keepdims=True)
    return (x3 * lax.rsqrt(ms + RMS_EPS)).reshape(D_ATTN, n) * g_col


def _inproj_prompt_kernel(x_ref, ng_ref, wc_ref, waz_ref, wgc_ref, wga_ref, cw_ref, wco_ref,
                          wqT_ref, wkT_ref, wvT_ref, wfT_ref, qg_ref, kg_ref, fbc_ref, triu_ref, b2_ref,
                          kT_ref, kb_ref, vT_ref, vTb_ref, qT_ref, qx_ref, ek_ref, lfT_ref, sz_ref, sga_ref, mc_ref,
                          cn_ref, ucar_ref, ccar_ref):
    @pl.when(pl.program_id(1) == 0)
    def _():
        ucar_ref[...] = jnp.zeros_like(ucar_ref)
        ccar_ref[...] = jnp.zeros_like(ccar_ref)

    x = x_ref[0]
    tm = x.shape[0]
    h, u, gate_c, sz, sgc, sga = _inproj_common(x, ng_ref, wc_ref, waz_ref, wgc_ref, wga_ref)

    rows = lax.broadcasted_iota(jnp.int32, u.shape, 0)
    prev1 = ucar_ref[SUBLANES - 1:SUBLANES, :]
    prev2 = ucar_ref[SUBLANES - 2:SUBLANES - 1, :]
    u1 = jnp.where(rows == 0, prev1, pltpu.roll(u, 1, 0))
    u2 = jnp.where(rows == 0, prev2, jnp.where(rows == 1, prev1, pltpu.roll(u, 2, 0)))
    cy = cw_ref[0:1, :] * u2 + cw_ref[1:2, :] * u1 + cw_ref[2:3, :] * u
    ucar_ref[...] = u[tm - SUBLANES:tm, :]
    cn_ref[0] = ucar_ref[SUBLANES - 2:SUBLANES, :]
    yc = _dot((gate_c * cy).astype(bf16), wco_ref[...])
    mc_ref[0] = (sgc * yc).astype(bf16)
    sga_ref[0] = sga
    sz_ref[0] = sz

    knT = _head_norm_T(_dot_nt(wkT_ref[...], h), kg_ref[...])
    kT_ref[0] = knT
    kb_ref[0] = knT.T.astype(bf16)
    vT = _dot_nt(wvT_ref[...], h)
    vT_ref[0] = vT
    vTb_ref[0] = vT.astype(bf16)
    qT_ref[0] = _head_norm_T(_dot_nt(wqT_ref[...], h), qg_ref[...]).astype(bf16)
    zT = _dot_nt(wfT_ref[...], h) + fbc_ref[...]
    hrow = lax.broadcasted_iota(jnp.int32, zT.shape, 0)
    lfT = jnp.where(hrow < N_HEADS, _log_sigmoid(zT), 0.0)
    lfT_ref[0] = lfT[0:N_HEADS, :]

    cT = ccar_ref[:, 0:1] + _dot3_lhs(lfT, triu_ref[...])
    ccar_ref[...] = jnp.broadcast_to(cT[:, tm - 1:tm], ccar_ref.shape)
    n_piece = 3 * N_HEADS
    khi, kmid, klo = [p.astype(f32)[0:N_HEADS] for p in _split3(cT * (-LOG2E))]
    shi, smid, slo = [p.astype(f32)[0:N_HEADS] for p in _split3(cT * LOG2E - b2_ref[0:1, 0:1])]
    ones = jnp.ones((n_piece, tm), f32)
    zeros = jnp.zeros((LANES - 2 * n_piece, tm), f32)
    ekT = jnp.concatenate([khi, kmid, klo, ones, zeros], axis=0)
    ek_ref[0] = ekT.T.astype(bf16)
    qx_ref[0] = jnp.concatenate([ones, shi, smid, slo, zeros], axis=0).astype(bf16)


def _head_norm(xf, g_row, bd):
    ssq = _dot((xf * xf).astype(bf16), bd)
    return xf * lax.rsqrt(ssq * (1.0 / HEAD_DIM) + RMS_EPS) * g_row


def _inproj_sample_kernel(x_ref, ng_ref, wc_ref, waz_ref, wgc_ref, wga_ref, cw_ref, wco_ref,
                          wq_ref, wk_ref, wv_ref, wf_ref, qg_ref, kg_ref, fb_ref, bd_ref, s1_ref, s2_ref,
                          k_ref, v_ref, q_ref, lf_ref, sz_ref, sga_ref, mc_ref, u_ref, *, dec_seq):
    x = x_ref[...]
    h, u, gate_c, sz, sgc, sga = _inproj_common(x, ng_ref, wc_ref, waz_ref, wgc_ref, wga_ref)
    t = lax.broadcasted_iota(jnp.int32, u.shape, 0) % dec_seq
    u1 = jnp.where(t == 0, s1_ref[...], pltpu.roll(u, 1, 0))
    u2 = jnp.where(t < 2, s2_ref[...], pltpu.roll(u, 2, 0))
    cy = cw_ref[0:1, :] * u2 + cw_ref[1:2, :] * u1 + cw_ref[2:3, :] * u
    u_ref[...] = u
    yc = _dot((gate_c * cy).astype(bf16), wco_ref[...])
    mc_ref[...] = (sgc * yc).astype(bf16)
    sga_ref[...] = sga
    sz_ref[...] = sz.astype(f32)
    k_ref[...] = _head_norm(_dot(h, wk_ref[...]), kg_ref[...], bd_ref[...])
    v_ref[...] = _dot(h, wv_ref[...])
    q_ref[...] = _head_norm(_dot(h, wq_ref[...]), qg_ref[...], bd_ref[...])
    lf_ref[...] = _log_sigmoid(_dot(h, wf_ref[...]) + fb_ref[...])[:, :N_HEADS]


def _attn_prompt_kernel(qT_ref, qx_ref, kb_ref, ek_ref, vT_ref, sz_ref, g_ref, qw_ref, m_ref, acc_ref, p_ref, pw_ref,
                        *, tile, fixed_shift):
    hp = pl.program_id(1)
    qi = pl.program_id(2)
    qT = qT_ref[0]
    qx = qx_ref[0]
    row = lax.broadcasted_iota(jnp.int32, qT.shape, 0)
    n_piece = 3 * N_HEADS
    for j in range(2):
        top = jnp.where((row // HEAD_DIM) == j, qT, jnp.zeros_like(qT))
        used = row < (2 * n_piece if fixed_shift else n_piece)
        bot = jnp.where(used & ((row % N_HEADS) == 2 * hp + j), qx, jnp.zeros_like(qx))
        qw_ref[:, j * tile:(j + 1) * tile] = jnp.concatenate([top, bot], axis=0)
    m_ref[...] = jnp.full_like(m_ref, NEG_INF)
    acc_ref[...] = jnp.zeros_like(acc_ref)
    ones = jnp.ones((acc_ref.shape[1] - HEAD_DIM, tile), bf16)

    def key_start(kt):
        return kt * tile if isinstance(kt, int) else pl.multiple_of(kt * tile, tile)

    def scores(kt, diagonal):
        k0 = key_start(kt)
        lhs = jnp.concatenate([kb_ref[0, pl.ds(k0, tile), :], ek_ref[0, pl.ds(k0, tile), :]], axis=1)
        s = _dot(lhs, qw_ref[...])
        if diagonal:
            kidx = lax.broadcasted_iota(jnp.int32, s.shape, 0)
            qidx = lax.broadcasted_iota(jnp.int32, s.shape, 1) % tile
            s = jnp.where(kidx <= qidx, s, NEG_INF)
        return s

    def values(kt, j):
        k0 = key_start(kt)
        return jnp.concatenate([vT_ref[0, j * HEAD_DIM:(j + 1) * HEAD_DIM, pl.ds(k0, tile)], ones], axis=0)

    if fixed_shift:
        def stage(kt, diagonal=False, first=False):
            if not first:
                pw_ref[...] = p_ref[...]
            p_ref[...] = jnp.exp2(scores(kt, diagonal)).astype(bf16)
            if not first:
                for j in range(2):
                    acc_ref[j] += _dot(values(kt - 1, j), pw_ref[:, j * tile:(j + 1) * tile])

        def stage_pair(i, carry):
            kt = 1 + 2 * i
            pw_ref[...] = jnp.exp2(scores(kt, False)).astype(bf16)
            for j in range(2):
                acc_ref[j] += _dot(values(kt - 1, j), p_ref[:, j * tile:(j + 1) * tile])
            p_ref[...] = jnp.exp2(scores(kt + 1, False)).astype(bf16)
            for j in range(2):
                acc_ref[j] += _dot(values(kt, j), pw_ref[:, j * tile:(j + 1) * tile])
            return carry

        @pl.when(qi > 0)
        def _():
            stage(0, first=True)
            lax.fori_loop(0, (qi - 1) // 2, stage_pair, 0)

            @pl.when((qi - 1) % 2 == 1)
            def _():
                stage(qi - 1)

            stage(qi, diagonal=True)

        @pl.when(qi == 0)
        def _():
            stage(0, diagonal=True, first=True)

        for j in range(2):
            acc_ref[j] += _dot(values(qi, j), p_ref[:, j * tile:(j + 1) * tile])
    else:
        def step(kt, diagonal):
            s2 = scores(kt, diagonal)
            for j in range(2):
                s = s2[:, j * tile:(j + 1) * tile]
                m_old = m_ref[j]
                m_new = jnp.maximum(m_old, jnp.max(s, axis=0, keepdims=True))
                pv = _dot(values(kt, j), jnp.exp2(s - m_new).astype(bf16))
                acc_ref[j] = jnp.exp2(m_old - m_new) * acc_ref[j] + pv
                m_ref[j] = m_new

        lax.fori_loop(0, qi, lambda kt, c: (step(kt, False), c)[1], 0)
        step(qi, True)
    o = jnp.concatenate([acc_ref[j, 0:HEAD_DIM, :] / acc_ref[j, HEAD_DIM:HEAD_DIM + 1, :] for j in range(2)], axis=0)
    g_ref[0] = (o.T * sz_ref[0].astype(f32)).astype(bf16)


def _logf_suffix_kernel(lp_ref, t_ref, o_ref):
    o_ref[...] = _dot3_lhs(lp_ref[...], t_ref[...])


def _attn_sample_kernel(pt_ref, q_ref, kn_ref, vn_ref, lfT_ref, sz_ref, ut_ref, *refs, dec_seq, n_pg):
    k_refs = refs[0:n_pg]
    v_refs = refs[n_pg:2 * n_pg]
    e_refs = refs[2 * n_pg:3 * n_pg]
    g_ref = refs[3 * n_pg]
    qbd_ref, kb_ref, vb_ref, run_ref, m_ref, l_ref, acc_ref = refs[3 * n_pg + 1:]
    c = pl.program_id(1)
    n_rows = dec_seq * N_HEADS

    def online_update(s, v_bf, v_feature_major):
        m_old = m_ref[...]
        m_new = jnp.maximum(m_old, jnp.max(s, axis=1, keepdims=True))
        alpha = jnp.exp2(m_old - m_new)
        p = jnp.exp2(s - m_new).astype(bf16)
        l_ref[...] = alpha * l_ref[...] + jnp.sum(p.astype(f32), axis=1, keepdims=True)
        pv = _dot_nt(p, v_bf) if v_feature_major else _dot(p, v_bf)
        acc_ref[...] = alpha * acc_ref[...] + pv
        m_ref[...] = m_new

    @pl.when(c == 0)
    def _():
        q = q_ref[...]
        q3 = jnp.broadcast_to(q[:, None, :], (dec_seq, N_HEADS, D_ATTN))
        hh = lax.broadcasted_iota(jnp.int32, q3.shape, 1)
        ln = lax.broadcasted_iota(jnp.int32, q3.shape, 2)
        qbd = jnp.where((ln // HEAD_DIM) == hh, q3, 0.0).reshape(n_rows, D_ATTN).astype(bf16)
        qbd_ref[...] = qbd
        m_ref[...] = jnp.full_like(m_ref, NEG_INF)
        l_ref[...] = jnp.zeros_like(l_ref)
        acc_ref[...] = jnp.zeros_like(acc_ref)
        run_ref[...] = jnp.zeros_like(run_ref)
        pad = jnp.zeros((LANES - dec_seq, D_ATTN), f32)
        kpad = jnp.concatenate([kn_ref[...], pad], axis=0).astype(bf16)
        vpad = jnp.concatenate([vn_ref[...], pad], axis=0).astype(bf16)
        s = _dot_nt(qbd, kpad)
        cnew = _dot3_lhs(lfT_ref[0], ut_ref[...])
        s3 = s.reshape(dec_seq, N_HEADS, LANES) - (cnew * LOG2E)[None]
        tt = lax.broadcasted_iota(jnp.int32, s3.shape, 0)
        jj = lax.broadcasted_iota(jnp.int32, s3.shape, 2)
        s3 = jnp.where(jj <= tt, s3, NEG_INF)
        online_update(s3.reshape(n_rows, LANES), vpad, False)

    bias = [None] * n_pg
    run = run_ref[...]
    for i in reversed(range(n_pg)):
        e = e_refs[i][0]
        bias[i] = (e[:, 0:LANES] + run) * LOG2E
        run = run + e[:, LANES:2 * LANES]
        kb_ref[:, i * LANES:(i + 1) * LANES] = k_refs[i][0].astype(bf16)
        vb_ref[:, i * LANES:(i + 1) * LANES] = v_refs[i][0].astype(bf16)
    run_ref[...] = run
    s = _dot(qbd_ref[...], kb_ref[...])
    b = jnp.concatenate(bias, axis=1)
    s3 = s.reshape(dec_seq, N_HEADS, n_pg * LANES) + b[None]
    online_update(s3.reshape(n_rows, n_pg * LANES), vb_ref[...], True)

    @pl.when(c == pl.num_programs(1) - 1)
    def _():
        o = acc_ref[...] / l_ref[...]
        o3 = o.reshape(dec_seq, N_HEADS, D_ATTN)
        hh = lax.broadcasted_iota(jnp.int32, o3.shape, 1)
        ln = lax.broadcasted_iota(jnp.int32, o3.shape, 2)
        o2 = jnp.sum(jnp.where((ln // HEAD_DIM) == hh, o3, 0.0), axis=1)
        g_ref[...] = o2 * sz_ref[...]


def _outproj_kernel(x_ref, g_ref, mc_ref, sga_ref, wao_ref, wo_ref, y_ref):
    ya = _dot(g_ref[...].astype(bf16), wao_ref[...])
    merged = mc_ref[...].astype(f32) + sga_ref[...].astype(f32) * ya
    y_ref[...] = x_ref[...] + _dot(merged.astype(bf16), wo_ref[...])


def _row_tile(n, pref):
    t = min(n, pref)
    while n % t:
        t //= 2
    return t


def _layer(xp, xs, cache_k, cache_v, cache_logf, state_conv, page_table,
           norm_g, w_in, conv_w, q_norm_g, k_norm_g, f_bias, w_conv_out, w_attn_out, w_o):
    B, S, D = xp.shape
    NS, T, _ = xs.shape
    n_phys = cache_k.shape[0]
    n_pages = page_table.shape[1]
    dc = conv_w.shape[1]
    tm = _row_tile(S, 512)
    tile = _row_tile(S, 512)
    n_pg = min(PAGES_PER_STEP, n_pages)
    cp = functools.partial(pltpu.CompilerParams, vmem_limit_bytes=VMEM_LIMIT)

    o0 = 4 * dc
    wb = w_in.astype(bf16)
    w_c = wb[:, 0:o0]
    w_q = wb[:, o0:o0 + D_ATTN]
    w_k = wb[:, o0 + D_ATTN:o0 + 2 * D_ATTN]
    w_v = wb[:, o0 + 2 * D_ATTN:o0 + 3 * D_ATTN]
    w_az = wb[:, o0 + 3 * D_ATTN:o0 + 4 * D_ATTN]
    o1 = o0 + 4 * D_ATTN
    w_f = jnp.pad(wb[:, o1:o1 + N_HEADS], ((0, 0), (0, LANES - N_HEADS)))
    w_gc = wb[:, o1 + N_HEADS:o1 + N_HEADS + D]
    w_ga = wb[:, o1 + N_HEADS + D:o1 + N_HEADS + 2 * D]
    ng = norm_g.reshape(1, D)
    kg = jnp.tile(k_norm_g, N_HEADS)
    qg = jnp.tile(q_norm_g, N_HEADS) * (HEAD_DIM ** -0.5 * LOG2E)
    fb = jnp.pad(f_bias, (0, LANES - N_HEADS))
    wco = w_conv_out.astype(bf16)
    wao = w_attn_out.astype(bf16)
    wo = w_o.astype(bf16)
    common_w = (ng, w_c, w_az, w_gc, w_ga, conv_w, wco)
    common_specs = [_const_spec(a.shape) for a in common_w]

    bound = SHIFT_MARGIN * HEAD_DIM * jnp.max(jnp.abs(qg)) * jnp.max(jnp.abs(kg))

    triu = jnp.asarray(np.triu(np.ones((tm, tm), np.float32)), bf16)
    w_fT = w_f.T[0:2 * SUBLANES]
    prompt_w = (w_q.T, w_k.T, w_v.T, w_fT, qg.reshape(D_ATTN, 1), kg.reshape(D_ATTN, 1),
                fb[0:2 * SUBLANES].reshape(2 * SUBLANES, 1), triu, jnp.full((1, LANES), bound, f32))
    nt = S // tm
    row_spec = lambda w: pl.BlockSpec((1, tm, w), lambda b, j: (b, j, 0))
    colT_spec = lambda r: pl.BlockSpec((1, r, tm), lambda b, j: (b, 0, j))
    kT_p, kb_p, vT_p, vTb_p, qT_p, qx_p, ek_p, lfT_p, sz_p, sga_p, mc_p, cn_p = pl.pallas_call(
        _inproj_prompt_kernel,
        grid=(B, nt),
        in_specs=[row_spec(D)] + common_specs + [_const_spec(a.shape) for a in prompt_w],
        out_specs=[colT_spec(D_ATTN), row_spec(D_ATTN), colT_spec(D_ATTN), colT_spec(D_ATTN), colT_spec(D_ATTN),
                   colT_spec(LANES), row_spec(LANES), colT_spec(N_HEADS), row_spec(D_ATTN), row_spec(D), row_spec(D),
                   pl.BlockSpec((1, CONV_WIDTH - 1, dc), lambda b, j: (b, 0, 0))],
        out_shape=[jax.ShapeDtypeStruct((B, D_ATTN, S), f32), jax.ShapeDtypeStruct((B, S, D_ATTN), bf16),
                   jax.ShapeDtypeStruct((B, D_ATTN, S), f32), jax.ShapeDtypeStruct((B, D_ATTN, S), bf16),
                   jax.ShapeDtypeStruct((B, D_ATTN, S), bf16), jax.ShapeDtypeStruct((B, LANES, S), bf16),
                   jax.ShapeDtypeStruct((B, S, LANES), bf16),
                   jax.ShapeDtypeStruct((B, N_HEADS, S), f32), jax.ShapeDtypeStruct((B, S, D_ATTN), bf16),
                   jax.ShapeDtypeStruct((B, S, D), bf16), jax.ShapeDtypeStruct((B, S, D), bf16),
                   jax.ShapeDtypeStruct((B, CONV_WIDTH - 1, dc), f32)],
        scratch_shapes=[pltpu.VMEM((SUBLANES, dc), f32), pltpu.VMEM((2 * SUBLANES, LANES), f32)],
        compiler_params=cp(dimension_semantics=("arbitrary", "arbitrary")),
        name="inproj_prompt",
    )(xp, *common_w, *prompt_w)

    nq = S // tile
    pair = 2 * HEAD_DIM

    def attn_prompt(fixed_shift, *ops):
        return pl.pallas_call(
            functools.partial(_attn_prompt_kernel, tile=tile, fixed_shift=fixed_shift),
            grid=(B, N_HEADS // 2, nq),
            in_specs=[pl.BlockSpec((1, pair, tile), lambda b, hp, qi: (b, hp, qi)),
                      pl.BlockSpec((1, LANES, tile), lambda b, hp, qi: (b, 0, qi)),
                      pl.BlockSpec((1, S, pair), lambda b, hp, qi: (b, 0, hp)),
                      pl.BlockSpec((1, S, LANES), lambda b, hp, qi: (b, 0, 0)),
                      pl.BlockSpec((1, pair, S), lambda b, hp, qi: (b, hp, 0)),
                      pl.BlockSpec((1, tile, pair), lambda b, hp, qi: (b, qi, hp))],
            out_specs=pl.BlockSpec((1, tile, pair), lambda b, hp, qi: (b, qi, hp)),
            out_shape=jax.ShapeDtypeStruct((B, S, D_ATTN), bf16),
            scratch_shapes=[pltpu.VMEM((2 * LANES, 2 * tile), bf16), pltpu.VMEM((2, 1, tile), f32),
                            pltpu.VMEM((2, HEAD_DIM + 2 * SUBLANES, tile), f32),
                            pltpu.VMEM((tile, 2 * tile), bf16), pltpu.VMEM((tile, 2 * tile), bf16)],
            compiler_params=cp(dimension_semantics=("arbitrary", "arbitrary", "arbitrary")),
            name="attn_prompt_fixed_shift" if fixed_shift else "attn_prompt_online_max",
        )(*ops)

    g_p = lax.cond(bound < MAX_FIXED_SHIFT_BOUND, functools.partial(attn_prompt, True),
                   functools.partial(attn_prompt, False), qT_p, qx_p, kb_p, ek_p, vTb_p, sz_p)

    def outproj(x2, g2, mc2, sga2, name):
        n = x2.shape[0]
        t = _row_tile(n, 512)
        rs = lambda w: pl.BlockSpec((t, w), lambda i: (i, 0))
        return pl.pallas_call(
            _outproj_kernel,
            grid=(n // t,),
            in_specs=[rs(D), rs(D_ATTN), rs(D), rs(D), _const_spec(wao.shape), _const_spec(wo.shape)],
            out_specs=rs(D),
            out_shape=jax.ShapeDtypeStruct((n, D), f32),
            compiler_params=cp(dimension_semantics=("arbitrary",)),
            name=name,
        )(x2, g2, mc2, sga2, wao, wo)

    y_p = outproj(xp.reshape(B * S, D), g_p.reshape(B * S, D_ATTN), mc_p.reshape(B * S, D),
                  sga_p.reshape(B * S, D), "outproj_prompt").reshape(B, S, D)

    R = NS * T
    ts = _row_tile(R, 512)
    zrow = jnp.zeros((NS, T - 1, dc), f32)
    s1 = jnp.concatenate([state_conv[:, 1:2], zrow], axis=1).reshape(R, dc)
    s2 = jnp.concatenate([state_conv, zrow[:, 1:]], axis=1).reshape(R, dc)
    head_of = np.arange(D_ATTN) // HEAD_DIM
    bd = jnp.asarray(head_of[:, None] == head_of[None, :], bf16)
    sample_w = (w_q, w_k, w_v, w_f, qg.reshape(1, D_ATTN), kg.reshape(1, D_ATTN), fb.reshape(1, LANES), bd)
    rs = lambda w: pl.BlockSpec((ts, w), lambda i: (i, 0))
    k_s, v_s, q_s, lf_s, sz_s, sga_s, mc_s, u_s = pl.pallas_call(
        functools.partial(_inproj_sample_kernel, dec_seq=T),
        grid=(R // ts,),
        in_specs=[rs(D)] + common_specs + [_const_spec(a.shape) for a in sample_w] + [rs(dc), rs(dc)],
        out_specs=[rs(D_ATTN), rs(D_ATTN), rs(D_ATTN), rs(N_HEADS), rs(D_ATTN), rs(D), rs(D), rs(dc)],
        out_shape=[jax.ShapeDtypeStruct((R, D_ATTN), f32), jax.ShapeDtypeStruct((R, D_ATTN), f32),
                   jax.ShapeDtypeStruct((R, D_ATTN), f32), jax.ShapeDtypeStruct((R, N_HEADS), f32),
                   jax.ShapeDtypeStruct((R, D_ATTN), f32), jax.ShapeDtypeStruct((R, D), bf16),
                   jax.ShapeDtypeStruct((R, D), bf16), jax.ShapeDtypeStruct((R, dc), f32)],
        compiler_params=cp(dimension_semantics=("arbitrary",)),
        name="inproj_sample",
    )(xs.reshape(R, D), *common_w, *sample_w, s1, s2)

    page = cache_logf.shape[1]
    assert page == LANES
    pos = np.arange(page)
    tmat = jnp.asarray(np.concatenate([pos[:, None] > pos[None, :], np.ones((page, page), bool)], axis=1), bf16)
    lp2 = jnp.transpose(cache_logf, (0, 2, 1)).reshape(n_phys * N_HEADS, page)
    tp = _row_tile(n_phys * N_HEADS, 2048)
    esuf = pl.pallas_call(
        _logf_suffix_kernel,
        grid=(n_phys * N_HEADS // tp,),
        in_specs=[pl.BlockSpec((tp, page), lambda i: (i, 0)), _const_spec(tmat.shape)],
        out_specs=pl.BlockSpec((tp, 2 * page), lambda i: (i, 0)),
        out_shape=jax.ShapeDtypeStruct((n_phys * N_HEADS, 2 * page), f32),
        compiler_params=cp(dimension_semantics=("arbitrary",)),
        name="logf_suffix",
    )(lp2, tmat).reshape(n_phys, N_HEADS, 2 * page)

    n_chunks = n_pages // n_pg
    lfT = jnp.pad(jnp.swapaxes(lf_s.reshape(NS, T, N_HEADS), 1, 2), ((0, 0), (0, 0), (0, LANES - T)))
    ut = jnp.asarray(np.triu(np.ones((LANES, LANES), np.float32)), bf16)
    ck2 = jnp.transpose(cache_k, (0, 2, 3, 1)).reshape(n_phys, D_ATTN, page)
    cv2 = jnp.transpose(cache_v, (0, 2, 3, 1)).reshape(n_phys, D_ATTN, page)

    def page_spec(shape, i):
        def imap(b, c, pt):
            return (pt[b, (n_chunks - 1 - c) * n_pg + i], 0, 0)
        return pl.BlockSpec(shape, imap)

    seq_spec = pl.BlockSpec((T, D_ATTN), lambda b, c, pt: (b, 0))
    g_s = pl.pallas_call(
        functools.partial(_attn_sample_kernel, dec_seq=T, n_pg=n_pg),
        grid_spec=pltpu.PrefetchScalarGridSpec(
            num_scalar_prefetch=1,
            grid=(NS, n_chunks),
            in_specs=[seq_spec, seq_spec, seq_spec,
                      pl.BlockSpec((1, N_HEADS, LANES), lambda b, c, pt: (b, 0, 0)),
                      seq_spec,
                      pl.BlockSpec((LANES, LANES), lambda b, c, pt: (0, 0))]
                     + [page_spec((1, D_ATTN, page), i) for i in range(n_pg)]
                     + [page_spec((1, D_ATTN, page), i) for i in range(n_pg)]
                     + [page_spec((1, N_HEADS, 2 * page), i) for i in range(n_pg)],
            out_specs=seq_spec,
            scratch_shapes=[pltpu.VMEM((T * N_HEADS, D_ATTN), bf16),
                            pltpu.VMEM((D_ATTN, n_pg * page), bf16), pltpu.VMEM((D_ATTN, n_pg * page), bf16),
                            pltpu.VMEM((N_HEADS, LANES), f32), pltpu.VMEM((T * N_HEADS, 1), f32),
                            pltpu.VMEM((T * N_HEADS, 1), f32), pltpu.VMEM((T * N_HEADS, D_ATTN), f32)]),
        out_shape=jax.ShapeDtypeStruct((R, D_ATTN), f32),
        compiler_params=cp(dimension_semantics=("arbitrary", "arbitrary")),
        name="attn_sample",
    )(page_table, q_s, k_s, v_s, lfT, sz_s, ut, *([ck2] * n_pg), *([cv2] * n_pg), *([esuf] * n_pg))

    y_s = outproj(xs.reshape(R, D), g_s, mc_s, sga_s, "outproj_sample").reshape(NS, T, D)

    unT = lambda a: jnp.transpose(a.reshape(B, N_HEADS, HEAD_DIM, S), (0, 3, 1, 2))
    return (y_p, y_s, unT(kT_p), unT(vT_p), jnp.transpose(lfT_p, (0, 2, 1)), cn_p,
            k_s.reshape(NS, T, N_HEADS, HEAD_DIM), v_s.reshape(NS, T, N_HEADS, HEAD_DIM),
            lf_s.reshape(NS, T, N_HEADS), u_s.reshape(NS, T, dc)[:, T - (CONV_WIDTH - 1):])


def kernel(x_prompt, x_sample, cache_k, cache_v, cache_logf, state_conv, page_table, norm_g, w_in, conv_w,
           q_norm_g, k_norm_g, f_bias, w_conv_out, w_attn_out, w_o):
    depth = norm_g.shape[0]
    xp, xs = x_prompt, x_sample
    outs = []
    for l in range(depth):
        res = _layer(xp, xs, cache_k[l], cache_v[l], cache_logf[l], state_conv[l], page_table,
                     norm_g[l], w_in[l], conv_w[l], q_norm_g[l], k_norm_g[l], f_bias[l],
                     w_conv_out[l], w_attn_out[l], w_o[l])
        xp, xs = res[0], res[1]
        outs.append(res[2:])
    stacked = tuple(jnp.stack([o[i] for o in outs]) for i in range(8))
    return (xp, xs) + stacked
```

```python
import functools

import jax
import jax.numpy as jnp
import numpy as np
from jax import lax
from jax.experimental import pallas as pl
from jax.experimental.pallas import tpu as pltpu

N_HEADS = 8
HEAD_DIM = 64
D_ATTN = N_HEADS * HEAD_DIM
CONV_WIDTH = 3
RMS_EPS = 1e-6
NEG_INF = -1e30
LOG2E = 1.4426950408889634
LANES = 128
SUBLANES = 8
VMEM_LIMIT = 56 * 1024 * 1024
PAGES_PER_STEP = 16
SHIFT_MARGIN = 1.02
MAX_FIXED_SHIFT_BOUND = 40.0

f32 = jnp.float32
bf16 = jnp.bfloat16


def _dot(a, b):
    return jnp.dot(a, b, preferred_element_type=f32)


def _dot_nt(a, b):
    return lax.dot_general(a, b, (((1,), (1,)), ((), ())), preferred_element_type=f32)


def _split3(a):
    hi = a.astype(bf16)
    r = a - hi.astype(f32)
    mid = r.astype(bf16)
    lo = (r - mid.astype(f32)).astype(bf16)
    return hi, mid, lo


def _dot3_lhs(a, b01):
    hi, mid, lo = _split3(a)
    return _dot(hi, b01) + _dot(mid, b01) + _dot(lo, b01)


def _log_sigmoid(z):
    return jnp.minimum(z, 0.0) - jnp.log1p(jnp.exp(-jnp.abs(z)))


def _silu(z):
    return z * jax.nn.sigmoid(z)


def _const_spec(shape):
    nd = len(shape)
    return pl.BlockSpec(shape, lambda *_: (0,) * nd, pipeline_mode=pl.Buffered(1))


def _inproj_common(x, ng_ref, wc_ref, waz_ref, wgc_ref, wga_ref):
    ms = jnp.mean(x * x, axis=-1, keepdims=True)
    h = (x * lax.rsqrt(ms + RMS_EPS) * ng_ref[...]).astype(bf16)
    dc = wc_ref.shape[1] // 4
    cb = _dot(h, wc_ref[:, 0 * dc:1 * dc])
    cc = _dot(h, wc_ref[:, 1 * dc:2 * dc])
    cx = _dot(h, wc_ref[:, 2 * dc:3 * dc])
    cz = _dot(h, wc_ref[:, 3 * dc:4 * dc])
    u = cc * cx
    gate_c = cb * _silu(cz)
    sz = _silu(_dot(h, waz_ref[...])).astype(bf16)
    sgc = jax.nn.sigmoid(_dot(h, wgc_ref[...]))
    sga = jax.nn.sigmoid(_dot(h, wga_ref[...])).astype(bf16)
    return h, u, gate_c, sz, sgc, sga


def _head_norm_T(xT, g_col):
    n = xT.shape[1]
    x3 = xT.reshape(N_HEADS, HEAD_DIM, n)
    ms = jnp.mean(x3 * x3, axis=1, keepdims=True)
    return (x3 * lax.rsqrt(ms + RMS_EPS)).reshape(D_ATTN, n) * g_col


def _inproj_prompt_kernel(x_ref, ng_ref, wc_ref, waz_ref, wgc_ref, wga_ref, cw_ref, wco_ref,
                          wqT_ref, wkT_ref, wvT_ref, wfT_ref, qg_ref, kg_ref, fbc_ref, triu_ref, b2_ref,
                          kT_ref, kb_ref, vT_ref, vTb_ref, qT_ref, qx_ref, ek_ref, lfT_ref, sz_ref, sga_ref, mc_ref,
                          cn_ref, ucar_ref, ccar_ref):
    @pl.when(pl.program_id(1) == 0)
    def _():
        ucar_ref[...] = jnp.zeros_like(ucar_ref)
        ccar_ref[...] = jnp.zeros_like(ccar_ref)

    x = x_ref[0]
    tm = x.shape[0]
    h, u, gate_c, sz, sgc, sga = _inproj_common(x, ng_ref, wc_ref, waz_ref, wgc_ref, wga_ref)

    rows = lax.broadcasted_iota(jnp.int32, u.shape, 0)
    prev1 = ucar_ref[SUBLANES - 1:SUBLANES, :]
    prev2 = ucar_ref[SUBLANES - 2:SUBLANES - 1, :]
    u1 = jnp.where(rows == 0, prev1, pltpu.roll(u, 1, 0))
    u2 = jnp.where(rows == 0, prev2, jnp.where(rows == 1, prev1, pltpu.roll(u, 2, 0)))
    cy = cw_ref[0:1, :] * u2 + cw_ref[1:2, :] * u1 + cw_ref[2:3, :] * u
    ucar_ref[...] = u[tm - SUBLANES:tm, :]
    cn_ref[0] = ucar_ref[SUBLANES - 2:SUBLANES, :]
    yc = _dot((gate_c * cy).astype(bf16), wco_ref[...])
    mc_ref[0] = (sgc * yc).astype(bf16)
    sga_ref[0] = sga
    sz_ref[0] = sz

    knT = _head_norm_T(_dot_nt(wkT_ref[...], h), kg_ref[...])
    kT_ref[0] = knT
    kb_ref[0] = knT.T.astype(bf16)
    vT = _dot_nt(wvT_ref[...], h)
    vT_ref[0] = vT
    vTb_ref[0] = vT.astype(bf16)
    qT_ref[0] = _head_norm_T(_dot_nt(wqT_ref[...], h), qg_ref[...]).astype(bf16)
    zT = _dot_nt(wfT_ref[...], h) + fbc_ref[...]
    hrow = lax.broadcasted_iota(jnp.int32, zT.shape, 0)
    lfT = jnp.where(hrow < N_HEADS, _log_sigmoid(zT), 0.0)
    lfT_ref[0] = lfT[0:N_HEADS, :]

    cT = ccar_ref[:, 0:1] + _dot3_lhs(lfT, triu_ref[...])
    ccar_ref[...] = jnp.broadcast_to(cT[:, tm - 1:tm], ccar_ref.shape)
    n_piece = 3 * N_HEADS
    khi, kmid, klo = [p.astype(f32)[0:N_HEADS] for p in _split3(cT * (-LOG2E))]
    shi, smid, slo = [p.astype(f32)[0:N_HEADS] for p in _split3(cT * LOG2E - b2_ref[0:1, 0:1])]
    ones = jnp.ones((n_piece, tm), f32)
    zeros = jnp.zeros((LANES - 2 * n_piece, tm), f32)
    ekT = jnp.concatenate([khi, kmid, klo, ones, zeros], axis=0)
    ek_ref[0] = ekT.T.astype(bf16)
    qx_ref[0] = jnp.concatenate([ones, shi, smid, slo, zeros], axis=0).astype(bf16)


def _head_norm(xf, g_row, bd):
    ssq = _dot((xf * xf).astype(bf16), bd)
    return xf * lax.rsqrt(ssq * (1.0 / HEAD_DIM) + RMS_EPS) * g_row


def _inproj_sample_kernel(x_ref, ng_ref, wc_ref, waz_ref, wgc_ref, wga_ref, cw_ref, wco_ref,
                          wq_ref, wk_ref, wv_ref, wf_ref, qg_ref, kg_ref, fb_ref, bd_ref, s1_ref, s2_ref,
                          k_ref, v_ref, q_ref, lf_ref, sz_ref, sga_ref, mc_ref, u_ref, *, dec_seq):
    x = x_ref[...]
    h, u, gate_c, sz, sgc, sga = _inproj_common(x, ng_ref, wc_ref, waz_ref, wgc_ref, wga_ref)
    t = lax.broadcasted_iota(jnp.int32, u.shape, 0) % dec_seq
    u1 = jnp.where(t == 0, s1_ref[...], pltpu.roll(u, 1, 0))
    u2 = jnp.where(t < 2, s2_ref[...], pltpu.roll(u, 2, 0))
    cy = cw_ref[0:1, :] * u2 + cw_ref[1:2, :] * u1 + cw_ref[2:3, :] * u
    u_ref[...] = u
    yc = _dot((gate_c * cy).astype(bf16), wco_ref[...])
    mc_ref[...] = (sgc * yc).astype(bf16)
    sga_ref[...] = sga
    sz_ref[...] = sz.astype(f32)
    k_ref[...] = _head_norm(_dot(h, wk_ref[...]), kg_ref[...], bd_ref[...])
    v_ref[...] = _dot(h, wv_ref[...])
    q_ref[...] = _head_norm(_dot(h, wq_ref[...]), qg_ref[...], bd_ref[...])
    lf_ref[...] = _log_sigmoid(_dot(h, wf_ref[...]) + fb_ref[...])[:, :N_HEADS]


def _attn_prompt_step(qT_ref, qx_ref, kb_ref, ek_ref, vT_ref, sz_ref, g_ref, qw_ref, m_ref, acc_ref, p_ref, pw_ref,
                      *, tile, fixed_shift, hp, qi):
    qT = qT_ref[0]
    qx = qx_ref[0]
    row = lax.broadcasted_iota(jnp.int32, qT.shape, 0)
    n_piece = 3 * N_HEADS
    for j in range(2):
        top = jnp.where((row // HEAD_DIM) == j, qT, jnp.zeros_like(qT))
        used = row < (2 * n_piece if fixed_shift else n_piece)
        bot = jnp.where(used & ((row % N_HEADS) == 2 * hp + j), qx, jnp.zeros_like(qx))
        qw_ref[:, j * tile:(j + 1) * tile] = jnp.concatenate([top, bot], axis=0)
    m_ref[...] = jnp.full_like(m_ref, NEG_INF)
    acc_ref[...] = jnp.zeros_like(acc_ref)
    ones = jnp.ones((acc_ref.shape[1] - HEAD_DIM, tile), bf16)

    def key_start(kt):
        return kt * tile if isinstance(kt, int) else pl.multiple_of(kt * tile, tile)

    def scores(kt, diagonal):
        k0 = key_start(kt)
        lhs = jnp.concatenate([kb_ref[0, pl.ds(k0, tile), :], ek_ref[0, pl.ds(k0, tile), :]], axis=1)
        s = _dot(lhs, qw_ref[...])
        if diagonal:
            kidx = lax.broadcasted_iota(jnp.int32, s.shape, 0)
            qidx = lax.broadcasted_iota(jnp.int32, s.shape, 1) % tile
            s = jnp.where(kidx <= qidx, s, NEG_INF)
        return s

    def values(kt, j):
        k0 = key_start(kt)
        return jnp.concatenate([vT_ref[0, j * HEAD_DIM:(j + 1) * HEAD_DIM, pl.ds(k0, tile)], ones], axis=0)

    if fixed_shift:
        def stage(kt, diagonal=False, first=False):
            if not first:
                pw_ref[...] = p_ref[...]
            p_ref[...] = jnp.exp2(scores(kt, diagonal)).astype(bf16)
            if not first:
                for j in range(2):
                    acc_ref[j] += _dot(values(kt - 1, j), pw_ref[:, j * tile:(j + 1) * tile])

        def stage_pair(i, carry):
            kt = 1 + 2 * i
            pw_ref[...] = jnp.exp2(scores(kt, False)).astype(bf16)
            for j in range(2):
                acc_ref[j] += _dot(values(kt - 1, j), p_ref[:, j * tile:(j + 1) * tile])
            p_ref[...] = jnp.exp2(scores(kt + 1, False)).astype(bf16)
            for j in range(2):
                acc_ref[j] += _dot(values(kt, j), pw_ref[:, j * tile:(j + 1) * tile])
            return carry

        @pl.when(qi > 0)
        def _():
            stage(0, first=True)
            lax.fori_loop(0, (qi - 1) // 2, stage_pair, 0)

            @pl.when((qi - 1) % 2 == 1)
            def _():
                stage(qi - 1)

            stage(qi, diagonal=True)

        @pl.when(qi == 0)
        def _():
            stage(0, diagonal=True, first=True)

        for j in range(2):
            acc_ref[j] += _dot(values(qi, j), p_ref[:, j * tile:(j + 1) * tile])
    else:
        def step(kt, diagonal):
            s2 = scores(kt, diagonal)
            for j in range(2):
                s = s2[:, j * tile:(j + 1) * tile]
                m_old = m_ref[j]
                m_new = jnp.maximum(m_old, jnp.max(s, axis=0, keepdims=True))
                pv = _dot(values(kt, j), jnp.exp2(s - m_new).astype(bf16))
                acc_ref[j] = jnp.exp2(m_old - m_new) * acc_ref[j] + pv
                m_ref[j] = m_new

        lax.fori_loop(0, qi, lambda kt, c: (step(kt, False), c)[1], 0)
        step(qi, True)
    o = jnp.concatenate([acc_ref[j, 0:HEAD_DIM, :] / acc_ref[j, HEAD_DIM:HEAD_DIM + 1, :] for j in range(2)], axis=0)
    g_ref[0] = (o.T * sz_ref[0].astype(f32)).astype(bf16)


def _logf_suffix_kernel(lp_ref, t_ref, o_ref):
    o_ref[...] = _dot3_lhs(lp_ref[...], t_ref[...])


def _attn_sample_step(q_ref, kn_ref, vn_ref, lfT_ref, sz_ref, ut_ref, k_refs, v_refs, e_refs, g_ref,
                      qbd_ref, kb_ref, vb_ref, run_ref, m_ref, l_ref, acc_ref, *, dec_seq, c, n_chunks):
    n_pg = len(k_refs)
    n_rows = dec_seq * N_HEADS

    def online_update(s, v_bf, v_feature_major):
        m_old = m_ref[...]
        m_new = jnp.maximum(m_old, jnp.max(s, axis=1, keepdims=True))
        alpha = jnp.exp2(m_old - m_new)
        p = jnp.exp2(s - m_new).astype(bf16)
        l_ref[...] = alpha * l_ref[...] + jnp.sum(p.astype(f32), axis=1, keepdims=True)
        pv = _dot_nt(p, v_bf) if v_feature_major else _dot(p, v_bf)
        acc_ref[...] = alpha * acc_ref[...] + pv
        m_ref[...] = m_new

    @pl.when(c == 0)
    def _():
        q = q_ref[...]
        q3 = jnp.broadcast_to(q[:, None, :], (dec_seq, N_HEADS, D_ATTN))
        hh = lax.broadcasted_iota(jnp.int32, q3.shape, 1)
        ln = lax.broadcasted_iota(jnp.int32, q3.shape, 2)
        qbd = jnp.where((ln // HEAD_DIM) == hh, q3, 0.0).reshape(n_rows, D_ATTN).astype(bf16)
        qbd_ref[...] = qbd
        m_ref[...] = jnp.full_like(m_ref, NEG_INF)
        l_ref[...] = jnp.zeros_like(l_ref)
        acc_ref[...] = jnp.zeros_like(acc_ref)
        run_ref[...] = jnp.zeros_like(run_ref)
        pad = jnp.zeros((LANES - dec_seq, D_ATTN), f32)
        kpad = jnp.concatenate([kn_ref[...], pad], axis=0).astype(bf16)
        vpad = jnp.concatenate([vn_ref[...], pad], axis=0).astype(bf16)
        s = _dot_nt(qbd, kpad)
        cnew = _dot3_lhs(lfT_ref[0], ut_ref[...])
        s3 = s.reshape(dec_seq, N_HEADS, LANES) - (cnew * LOG2E)[None]
        tt = lax.broadcasted_iota(jnp.int32, s3.shape, 0)
        jj = lax.broadcasted_iota(jnp.int32, s3.shape, 2)
        s3 = jnp.where(jj <= tt, s3, NEG_INF)
        online_update(s3.reshape(n_rows, LANES), vpad, False)

    bias = [None] * n_pg
    run = run_ref[...]
    for i in reversed(range(n_pg)):
        e = e_refs[i][...]
        bias[i] = (e[:, 0:LANES] + run) * LOG2E
        run = run + e[:, LANES:2 * LANES]
        kb_ref[:, i * LANES:(i + 1) * LANES] = k_refs[i][...].astype(bf16)
        vb_ref[:, i * LANES:(i + 1) * LANES] = v_refs[i][...].astype(bf16)
    run_ref[...] = run
    s = _dot(qbd_ref[...], kb_ref[...])
    b = jnp.concatenate(bias, axis=1)
    s3 = s.reshape(dec_seq, N_HEADS, n_pg * LANES) + b[None]
    online_update(s3.reshape(n_rows, n_pg * LANES), vb_ref[...], True)

    @pl.when(c == n_chunks - 1)
    def _():
        o = acc_ref[...] / l_ref[...]
        o3 = o.reshape(dec_seq, N_HEADS, D_ATTN)
        hh = lax.broadcasted_iota(jnp.int32, o3.shape, 1)
        ln = lax.broadcasted_iota(jnp.int32, o3.shape, 2)
        o2 = jnp.sum(jnp.where((ln // HEAD_DIM) == hh, o3, 0.0), axis=1)
        g_ref[...] = o2 * sz_ref[...]


N_PROMPT_IN = 6
N_PROMPT_SCRATCH = 5
N_SAMPLE_IN = 6
N_SAMPLE_SCRATCH = 7


def _attn_kernel(pt_ref, *refs, tile, fixed_shift, dec_seq, n_pg, n_chunks, n_sample_steps):
    refs = list(refs)
    take = lambda n: [refs.pop(0) for _ in range(n)]
    prompt_in, sample_in = take(N_PROMPT_IN), take(N_SAMPLE_IN)
    cache_refs = take(3)
    g_ref, gs_ref = take(2)
    prompt_scratch, sample_scratch = take(N_PROMPT_SCRATCH), take(N_SAMPLE_SCRATCH)
    page_bufs = take(3)
    (sem,) = take(1)
    assert not refs
    b, hp, qi = pl.program_id(0), pl.program_id(1), pl.program_id(2)
    step = (b * pl.num_programs(1) + hp) * pl.num_programs(2) + qi

    def page_copies(m, slot):
        seq = m // n_chunks
        first_page = (n_chunks - 1 - m % n_chunks) * n_pg
        copies = []
        for i in range(n_pg):
            pg = pt_ref[seq, first_page + i]
            for a, (src, dst) in enumerate(zip(cache_refs, page_bufs)):
                copies.append(pltpu.make_async_copy(src.at[pg], dst.at[slot, i], sem.at[slot, a]))
        return copies

    @pl.when(step == 0)
    def _():
        for cp in page_copies(0, 0):
            cp.start()

    @pl.when(step + 1 < n_sample_steps)
    def _():
        for cp in page_copies(step + 1, (step + 1) % 2):
            cp.start()

    @pl.when(step < n_sample_steps)
    def _():
        slot = step % 2
        for cp in page_copies(step, slot):
            cp.wait()
        pages = [[buf.at[slot, i] for i in range(n_pg)] for buf in page_bufs]
        _attn_sample_step(*sample_in, *pages, gs_ref, *sample_scratch,
                          dec_seq=dec_seq, c=step % n_chunks, n_chunks=n_chunks)

    _attn_prompt_step(*prompt_in, g_ref, *prompt_scratch, tile=tile, fixed_shift=fixed_shift, hp=hp, qi=qi)


def _outproj_kernel(x_ref, g_ref, mc_ref, sga_ref, wao_ref, wo_ref, y_ref):
    ya = _dot(g_ref[...].astype(bf16), wao_ref[...])
    merged = mc_ref[...].astype(f32) + sga_ref[...].astype(f32) * ya
    y_ref[...] = x_ref[...] + _dot(merged.astype(bf16), wo_ref[...])


def _row_tile(n, pref):
    t = min(n, pref)
    while n % t:
        t //= 2
    return t


def _layer(xp, xs, cache_k, cache_v, cache_logf, state_conv, page_table,
           norm_g, w_in, conv_w, q_norm_g, k_norm_g, f_bias, w_conv_out, w_attn_out, w_o):
    B, S, D = xp.shape
    NS, T, _ = xs.shape
    n_phys = cache_k.shape[0]
    n_pages = page_table.shape[1]
    dc = conv_w.shape[1]
    tm = _row_tile(S, 512)
    tile = _row_tile(S, 512)
    n_pg = min(PAGES_PER_STEP, n_pages)
    cp = functools.partial(pltpu.CompilerParams, vmem_limit_bytes=VMEM_LIMIT)

    o0 = 4 * dc
    wb = w_in.astype(bf16)
    w_c = wb[:, 0:o0]
    w_q = wb[:, o0:o0 + D_ATTN]
    w_k = wb[:, o0 + D_ATTN:o0 + 2 * D_ATTN]
    w_v = wb[:, o0 + 2 * D_ATTN:o0 + 3 * D_ATTN]
    w_az = wb[:, o0 + 3 * D_ATTN:o0 + 4 * D_ATTN]
    o1 = o0 + 4 * D_ATTN
    w_f = jnp.pad(wb[:, o1:o1 + N_HEADS], ((0, 0), (0, LANES - N_HEADS)))
    w_gc = wb[:, o1 + N_HEADS:o1 + N_HEADS + D]
    w_ga = wb[:, o1 + N_HEADS + D:o1 + N_HEADS + 2 * D]
    ng = norm_g.reshape(1, D)
    kg = jnp.tile(k_norm_g, N_HEADS)
    qg = jnp.tile(q_norm_g, N_HEADS) * (HEAD_DIM ** -0.5 * LOG2E)
    fb = jnp.pad(f_bias, (0, LANES - N_HEADS))
    wco = w_conv_out.astype(bf16)
    wao = w_attn_out.astype(bf16)
    wo = w_o.astype(bf16)
    common_w = (ng, w_c, w_az, w_gc, w_ga, conv_w, wco)
    common_specs = [_const_spec(a.shape) for a in common_w]

    bound = SHIFT_MARGIN * HEAD_DIM * jnp.max(jnp.abs(qg)) * jnp.max(jnp.abs(kg))

    triu = jnp.asarray(np.triu(np.ones((tm, tm), np.float32)), bf16)
    w_fT = w_f.T[0:2 * SUBLANES]
    prompt_w = (w_q.T, w_k.T, w_v.T, w_fT, qg.reshape(D_ATTN, 1), kg.reshape(D_ATTN, 1),
                fb[0:2 * SUBLANES].reshape(2 * SUBLANES, 1), triu, jnp.full((1, LANES), bound, f32))
    nt = S // tm
    row_spec = lambda w: pl.BlockSpec((1, tm, w), lambda b, j: (b, j, 0))
    colT_spec = lambda r: pl.BlockSpec((1, r, tm), lambda b, j: (b, 0, j))
    kT_p, kb_p, vT_p, vTb_p, qT_p, qx_p, ek_p, lfT_p, sz_p, sga_p, mc_p, cn_p = pl.pallas_call(
        _inproj_prompt_kernel,
        grid=(B, nt),
        in_specs=[row_spec(D)] + common_specs + [_const_spec(a.shape) for a in prompt_w],
        out_specs=[colT_spec(D_ATTN), row_spec(D_ATTN), colT_spec(D_ATTN), colT_spec(D_ATTN), colT_spec(D_ATTN),
                   colT_spec(LANES), row_spec(LANES), colT_spec(N_HEADS), row_spec(D_ATTN), row_spec(D), row_spec(D),
                   pl.BlockSpec((1, CONV_WIDTH - 1, dc), lambda b, j: (b, 0, 0))],
        out_shape=[jax.ShapeDtypeStruct((B, D_ATTN, S), f32), jax.ShapeDtypeStruct((B, S, D_ATTN), bf16),
                   jax.ShapeDtypeStruct((B, D_ATTN, S), f32), jax.ShapeDtypeStruct((B, D_ATTN, S), bf16),
                   jax.ShapeDtypeStruct((B, D_ATTN, S), bf16), jax.ShapeDtypeStruct((B, LANES, S), bf16),
                   jax.ShapeDtypeStruct((B, S, LANES), bf16),
                   jax.ShapeDtypeStruct((B, N_HEADS, S), f32), jax.ShapeDtypeStruct((B, S, D_ATTN), bf16),
                   jax.ShapeDtypeStruct((B, S, D), bf16), jax.ShapeDtypeStruct((B, S, D), bf16),
                   jax.ShapeDtypeStruct((B, CONV_WIDTH - 1, dc), f32)],
        scratch_shapes=[pltpu.VMEM((SUBLANES, dc), f32), pltpu.VMEM((2 * SUBLANES, LANES), f32)],
        compiler_params=cp(dimension_semantics=("arbitrary", "arbitrary")),
        name="inproj_prompt",
    )(xp, *common_w, *prompt_w)

    R = NS * T
    ts = _row_tile(R, 512)
    zrow = jnp.zeros((NS, T - 1, dc), f32)
    s1 = jnp.concatenate([state_conv[:, 1:2], zrow], axis=1).reshape(R, dc)
    s2 = jnp.concatenate([state_conv, zrow[:, 1:]], axis=1).reshape(R, dc)
    head_of = np.arange(D_ATTN) // HEAD_DIM
    bd = jnp.asarray(head_of[:, None] == head_of[None, :], bf16)
    sample_w = (w_q, w_k, w_v, w_f, qg.reshape(1, D_ATTN), kg.reshape(1, D_ATTN), fb.reshape(1, LANES), bd)
    rs = lambda w: pl.BlockSpec((ts, w), lambda i: (i, 0))
    k_s, v_s, q_s, lf_s, sz_s, sga_s, mc_s, u_s = pl.pallas_call(
        functools.partial(_inproj_sample_kernel, dec_seq=T),
        grid=(R // ts,),
        in_specs=[rs(D)] + common_specs + [_const_spec(a.shape) for a in sample_w] + [rs(dc), rs(dc)],
        out_specs=[rs(D_ATTN), rs(D_ATTN), rs(D_ATTN), rs(N_HEADS), rs(D_ATTN), rs(D), rs(D), rs(dc)],
        out_shape=[jax.ShapeDtypeStruct((R, D_ATTN), f32), jax.ShapeDtypeStruct((R, D_ATTN), f32),
                   jax.ShapeDtypeStruct((R, D_ATTN), f32), jax.ShapeDtypeStruct((R, N_HEADS), f32),
                   jax.ShapeDtypeStruct((R, D_ATTN), f32), jax.ShapeDtypeStruct((R, D), bf16),
                   jax.ShapeDtypeStruct((R, D), bf16), jax.ShapeDtypeStruct((R, dc), f32)],
        compiler_params=cp(dimension_semantics=("arbitrary",)),
        name="inproj_sample",
    )(xs.reshape(R, D), *common_w, *sample_w, s1, s2)

    page = cache_logf.shape[1]
    assert page == LANES
    pos = np.arange(page)
    tmat = jnp.asarray(np.concatenate([pos[:, None] > pos[None, :], np.ones((page, page), bool)], axis=1), bf16)
    lp2 = jnp.transpose(cache_logf, (0, 2, 1)).reshape(n_phys * N_HEADS, page)
    tp = _row_tile(n_phys * N_HEADS, 2048)
    esuf = pl.pallas_call(
        _logf_suffix_kernel,
        grid=(n_phys * N_HEADS // tp,),
        in_specs=[pl.BlockSpec((tp, page), lambda i: (i, 0)), _const_spec(tmat.shape)],
        out_specs=pl.BlockSpec((tp, 2 * page), lambda i: (i, 0)),
        out_shape=jax.ShapeDtypeStruct((n_phys * N_HEADS, 2 * page), f32),
        compiler_params=cp(dimension_semantics=("arbitrary",)),
        name="logf_suffix",
    )(lp2, tmat).reshape(n_phys, N_HEADS, 2 * page)

    nq = S // tile
    pair = 2 * HEAD_DIM
    n_prompt_steps = B * (N_HEADS // 2) * nq
    while NS * (n_pages // n_pg) > n_prompt_steps and n_pg < n_pages:
        n_pg *= 2
    n_chunks = n_pages // n_pg
    n_sample_steps = NS * n_chunks
    if n_pages % n_pg or n_sample_steps > n_prompt_steps:
        raise NotImplementedError("decode steps must fit in the prompt attention grid")
    lfT = jnp.pad(jnp.swapaxes(lf_s.reshape(NS, T, N_HEADS), 1, 2), ((0, 0), (0, 0), (0, LANES - T)))
    ut = jnp.asarray(np.triu(np.ones((LANES, LANES), np.float32)), bf16)
    ck2 = jnp.transpose(cache_k, (0, 2, 3, 1)).reshape(n_phys, D_ATTN, page)
    cv2 = jnp.transpose(cache_v, (0, 2, 3, 1)).reshape(n_phys, D_ATTN, page)

    def decode_step(b, hp, qi):
        return jnp.minimum((b * (N_HEADS // 2) + hp) * nq + qi, n_sample_steps - 1)

    seq_spec = pl.BlockSpec((T, D_ATTN), lambda b, hp, qi, pt: (decode_step(b, hp, qi) // n_chunks, 0))
    prompt_specs = [pl.BlockSpec((1, pair, tile), lambda b, hp, qi, pt: (b, hp, qi)),
                    pl.BlockSpec((1, LANES, tile), lambda b, hp, qi, pt: (b, 0, qi)),
                    pl.BlockSpec((1, S, pair), lambda b, hp, qi, pt: (b, 0, hp)),
                    pl.BlockSpec((1, S, LANES), lambda b, hp, qi, pt: (b, 0, 0)),
                    pl.BlockSpec((1, pair, S), lambda b, hp, qi, pt: (b, hp, 0)),
                    pl.BlockSpec((1, tile, pair), lambda b, hp, qi, pt: (b, qi, hp))]
    sample_specs = [seq_spec, seq_spec, seq_spec,
                    pl.BlockSpec((1, N_HEADS, LANES), lambda b, hp, qi, pt: (decode_step(b, hp, qi) // n_chunks, 0, 0)),
                    seq_spec,
                    pl.BlockSpec((LANES, LANES), lambda b, hp, qi, pt: (0, 0))]
    prompt_scratch = [pltpu.VMEM((2 * LANES, 2 * tile), bf16), pltpu.VMEM((2, 1, tile), f32),
                      pltpu.VMEM((2, HEAD_DIM + 2 * SUBLANES, tile), f32),
                      pltpu.VMEM((tile, 2 * tile), bf16), pltpu.VMEM((tile, 2 * tile), bf16)]
    sample_scratch = [pltpu.VMEM((T * N_HEADS, D_ATTN), bf16),
                      pltpu.VMEM((D_ATTN, n_pg * page), bf16), pltpu.VMEM((D_ATTN, n_pg * page), bf16),
                      pltpu.VMEM((N_HEADS, LANES), f32), pltpu.VMEM((T * N_HEADS, 1), f32),
                      pltpu.VMEM((T * N_HEADS, 1), f32), pltpu.VMEM((T * N_HEADS, D_ATTN), f32)]
    page_scratch = [pltpu.VMEM((2, n_pg, D_ATTN, page), f32), pltpu.VMEM((2, n_pg, D_ATTN, page), f32),
                    pltpu.VMEM((2, n_pg, N_HEADS, 2 * page), f32), pltpu.SemaphoreType.DMA((2, 3))]
    assert (len(prompt_specs), len(sample_specs)) == (N_PROMPT_IN, N_SAMPLE_IN)
    assert (len(prompt_scratch), len(sample_scratch)) == (N_PROMPT_SCRATCH, N_SAMPLE_SCRATCH)

    def attention(fixed_shift, *ops):
        return pl.pallas_call(
            functools.partial(_attn_kernel, tile=tile, fixed_shift=fixed_shift, dec_seq=T, n_pg=n_pg,
                              n_chunks=n_chunks, n_sample_steps=n_sample_steps),
            grid_spec=pltpu.PrefetchScalarGridSpec(
                num_scalar_prefetch=1,
                grid=(B, N_HEADS // 2, nq),
                in_specs=prompt_specs + sample_specs + [pl.BlockSpec(memory_space=pl.ANY)] * 3,
                out_specs=[pl.BlockSpec((1, tile, pair), lambda b, hp, qi, pt: (b, qi, hp)), seq_spec],
                scratch_shapes=prompt_scratch + sample_scratch + page_scratch),
            out_shape=[jax.ShapeDtypeStruct((B, S, D_ATTN), bf16), jax.ShapeDtypeStruct((R, D_ATTN), f32)],
            compiler_params=cp(dimension_semantics=("arbitrary", "arbitrary", "arbitrary")),
            name="attn_fixed_shift" if fixed_shift else "attn_online_max",
        )(*ops)

    g_p, g_s = lax.cond(bound < MAX_FIXED_SHIFT_BOUND, functools.partial(attention, True),
                        functools.partial(attention, False),
                        page_table, qT_p, qx_p, kb_p, ek_p, vTb_p, sz_p, q_s, k_s, v_s, lfT, sz_s, ut, ck2, cv2, esuf)

    def outproj(x2, g2, mc2, sga2, name):
        n = x2.shape[0]
        t = _row_tile(n, 512)
        rs = lambda w: pl.BlockSpec((t, w), lambda i: (i, 0))
        return pl.pallas_call(
            _outproj_kernel,
            grid=(n // t,),
            in_specs=[rs(D), rs(D_ATTN), rs(D), rs(D), _const_spec(wao.shape), _const_spec(wo.shape)],
            out_specs=rs(D),
            out_shape=jax.ShapeDtypeStruct((n, D), f32),
            compiler_params=cp(dimension_semantics=("arbitrary",)),
            name=name,
        )(x2, g2, mc2, sga2, wao, wo)

    y_p = outproj(xp.reshape(B * S, D), g_p.reshape(B * S, D_ATTN), mc_p.reshape(B * S, D),
                  sga_p.reshape(B * S, D), "outproj_prompt").reshape(B, S, D)
    y_s = outproj(xs.reshape(R, D), g_s, mc_s, sga_s, "outproj_sample").reshape(NS, T, D)

    unT = lambda a: jnp.transpose(a.reshape(B, N_HEADS, HEAD_DIM, S), (0, 3, 1, 2))
    return (y_p, y_s, unT(kT_p), unT(vT_p), jnp.transpose(lfT_p, (0, 2, 1)), cn_p,
            k_s.reshape(NS, T, N_HEADS, HEAD_DIM), v_s.reshape(NS, T, N_HEADS, HEAD_DIM),
            lf_s.reshape(NS, T, N_HEADS), u_s.reshape(NS, T, dc)[:, T - (CONV_WIDTH - 1):])


def kernel(x_prompt, x_sample, cache_k, cache_v, cache_logf, state_conv, page_table, norm_g, w_in, conv_w,
           q_norm_g, k_norm_g, f_bias, w_conv_out, w_attn_out, w_o):
    depth = norm_g.shape[0]
    xp, xs = x_prompt, x_sample
    outs = []
    for l in range(depth):
        res = _layer(xp, xs, cache_k[l], cache_v[l], cache_logf[l], state_conv[l], page_table,
                     norm_g[l], w_in[l], conv_w[l], q_norm_g[l], k_norm_g[l], f_bias[l],
                     w_conv_out[l], w_attn_out[l], w_o[l])
        xp, xs = res[0], res[1]
        outs.append(res[2:])
    stacked = tuple(jnp.stack([o[i] for o in outs]) for i in range(8))
    return (xp, xs) + stacked
```

```python
import functools

import jax
import jax.numpy as jnp
import numpy as np
from jax import lax
from jax.experimental import pallas as pl
from jax.experimental.pallas import tpu as pltpu

N_HEADS = 8
HEAD_DIM = 64
D_ATTN = N_HEADS * HEAD_DIM
CONV_WIDTH = 3
RMS_EPS = 1e-6
NEG_INF = -1e30
LOG2E = 1.4426950408889634
LANES = 128
SUBLANES = 8
VMEM_LIMIT = 56 * 1024 * 1024
PAGES_PER_STEP = 16
TILES_PER_TRIP = 4
SHIFT_MARGIN = 1.02
MAX_FIXED_SHIFT_BOUND = 40.0

f32 = jnp.float32
bf16 = jnp.bfloat16


def _dot(a, b):
    return jnp.dot(a, b, preferred_element_type=f32)


def _dot_nt(a, b):
    return lax.dot_general(a, b, (((1,), (1,)), ((), ())), preferred_element_type=f32)


def _split3(a):
    hi = a.astype(bf16)
    r = a - hi.astype(f32)
    mid = r.astype(bf16)
    lo = (r - mid.astype(f32)).astype(bf16)
    return hi, mid, lo


def _dot3_lhs(a, b01):
    hi, mid, lo = _split3(a)
    return _dot(hi, b01) + _dot(mid, b01) + _dot(lo, b01)


def _log_sigmoid(z):
    return jnp.minimum(z, 0.0) - jnp.log1p(jnp.exp(-jnp.abs(z)))


def _silu(z):
    return z * jax.nn.sigmoid(z)


def _const_spec(shape):
    nd = len(shape)
    return pl.BlockSpec(shape, lambda *_: (0,) * nd, pipeline_mode=pl.Buffered(1))


def _inproj_common(x, ng_ref, wc_ref, waz_ref, wgc_ref, wga_ref):
    ms = jnp.mean(x * x, axis=-1, keepdims=True)
    h = (x * lax.rsqrt(ms + RMS_EPS) * ng_ref[...]).astype(bf16)
    dc = wc_ref.shape[1] // 4
    cb = _dot(h, wc_ref[:, 0 * dc:1 * dc])
    cc = _dot(h, wc_ref[:, 1 * dc:2 * dc])
    cx = _dot(h, wc_ref[:, 2 * dc:3 * dc])
    cz = _dot(h, wc_ref[:, 3 * dc:4 * dc])
    u = cc * cx
    gate_c = cb * _silu(cz)
    sz = _silu(_dot(h, waz_ref[...])).astype(bf16)
    sgc = jax.nn.sigmoid(_dot(h, wgc_ref[...]))
    sga = jax.nn.sigmoid(_dot(h, wga_ref[...])).astype(bf16)
    return h, u, gate_c, sz, sgc, sga


def _head_norm_T(xT, g_col):
    n = xT.shape[1]
    x3 = xT.reshape(N_HEADS, HEAD_DIM, n)
    ms = jnp.mean(x3 * x3, axis=1, keepdims=True)
    return (x3 * lax.rsqrt(ms + RMS_EPS)).reshape(D_ATTN, n) * g_col


def _inproj_prompt_kernel(x_ref, ng_ref, wc_ref, waz_ref, wgc_ref, wga_ref, cw_ref, wco_ref,
                          wqT_ref, wkT_ref, wvT_ref, wfT_ref, qg_ref, kg_ref, fbc_ref, triu_ref, b2_ref,
                          kT_ref, kb_ref, vT_ref, vTb_ref, qT_ref, qx_ref, ek_ref, lfT_ref, sz_ref, sga_ref, mc_ref,
                          cn_ref, ucar_ref, ccar_ref):
    @pl.when(pl.program_id(1) == 0)
    def _():
        ucar_ref[...] = jnp.zeros_like(ucar_ref)
        ccar_ref[...] = jnp.zeros_like(ccar_ref)

    x = x_ref[0]
    tm = x.shape[0]
    h, u, gate_c, sz, sgc, sga = _inproj_common(x, ng_ref, wc_ref, waz_ref, wgc_ref, wga_ref)

    rows = lax.broadcasted_iota(jnp.int32, u.shape, 0)
    prev1 = ucar_ref[SUBLANES - 1:SUBLANES, :]
    prev2 = ucar_ref[SUBLANES - 2:SUBLANES - 1, :]
    u1 = jnp.where(rows == 0, prev1, pltpu.roll(u, 1, 0))
    u2 = jnp.where(rows == 0, prev2, jnp.where(rows == 1, prev1, pltpu.roll(u, 2, 0)))
    cy = cw_ref[0:1, :] * u2 + cw_ref[1:2, :] * u1 + cw_ref[2:3, :] * u
    ucar_ref[...] = u[tm - SUBLANES:tm, :]
    cn_ref[0] = ucar_ref[SUBLANES - 2:SUBLANES, :]
    yc = _dot((gate_c * cy).astype(bf16), wco_ref[...])
    mc_ref[0] = (sgc * yc).astype(bf16)
    sga_ref[0] = sga
    sz_ref[0] = sz

    knT = _head_norm_T(_dot_nt(wkT_ref[...], h), kg_ref[...])
    kT_ref[0] = knT
    kb_ref[0] = knT.T.astype(bf16)
    vT = _dot_nt(wvT_ref[...], h)
    vT_ref[0] = vT
    vTb_ref[0] = vT.astype(bf16)
    qT_ref[0] = _head_norm_T(_dot_nt(wqT_ref[...], h), qg_ref[...]).astype(bf16)
    zT = _dot_nt(wfT_ref[...], h) + fbc_ref[...]
    hrow = lax.broadcasted_iota(jnp.int32, zT.shape, 0)
    lfT = jnp.where(hrow < N_HEADS, _log_sigmoid(zT), 0.0)
    lfT_ref[0] = lfT[0:N_HEADS, :]

    cT = ccar_ref[:, 0:1] + _dot3_lhs(lfT, triu_ref[...])
    ccar_ref[...] = jnp.broadcast_to(cT[:, tm - 1:tm], ccar_ref.shape)
    n_piece = 3 * N_HEADS
    khi, kmid, klo = [p.astype(f32)[0:N_HEADS] for p in _split3(cT * (-LOG2E))]
    shi, smid, slo = [p.astype(f32)[0:N_HEADS] for p in _split3(cT * LOG2E - b2_ref[0:1, 0:1])]
    ones = jnp.ones((n_piece, tm), f32)
    zeros = jnp.zeros((LANES - 2 * n_piece, tm), f32)
    ekT = jnp.concatenate([khi, kmid, klo, ones, zeros], axis=0)
    ek_ref[0] = ekT.T.astype(bf16)
    qx_ref[0] = jnp.concatenate([ones, shi, smid, slo, zeros], axis=0).astype(bf16)


def _head_norm(xf, g_row, bd):
    ssq = _dot((xf * xf).astype(bf16), bd)
    return xf * lax.rsqrt(ssq * (1.0 / HEAD_DIM) + RMS_EPS) * g_row


def _inproj_sample_kernel(x_ref, ng_ref, wc_ref, waz_ref, wgc_ref, wga_ref, cw_ref, wco_ref,
                          wq_ref, wk_ref, wv_ref, wf_ref, qg_ref, kg_ref, fb_ref, bd_ref, s1_ref, s2_ref,
                          k_ref, v_ref, q_ref, lf_ref, sz_ref, sga_ref, mc_ref, u_ref, *, dec_seq):
    x = x_ref[...]
    h, u, gate_c, sz, sgc, sga = _inproj_common(x, ng_ref, wc_ref, waz_ref, wgc_ref, wga_ref)
    t = lax.broadcasted_iota(jnp.int32, u.shape, 0) % dec_seq
    u1 = jnp.where(t == 0, s1_ref[...], pltpu.roll(u, 1, 0))
    u2 = jnp.where(t < 2, s2_ref[...], pltpu.roll(u, 2, 0))
    cy = cw_ref[0:1, :] * u2 + cw_ref[1:2, :] * u1 + cw_ref[2:3, :] * u
    u_ref[...] = u
    yc = _dot((gate_c * cy).astype(bf16), wco_ref[...])
    mc_ref[...] = (sgc * yc).astype(bf16)
    sga_ref[...] = sga
    sz_ref[...] = sz.astype(f32)
    k_ref[...] = _head_norm(_dot(h, wk_ref[...]), kg_ref[...], bd_ref[...])
    v_ref[...] = _dot(h, wv_ref[...])
    q_ref[...] = _head_norm(_dot(h, wq_ref[...]), qg_ref[...], bd_ref[...])
    lf_ref[...] = _log_sigmoid(_dot(h, wf_ref[...]) + fb_ref[...])[:, :N_HEADS]


def _attn_prompt_step(qT_ref, qx_ref, kb_ref, ek_ref, vT_ref, sz_ref, g_ref, qw_ref, m_ref, acc_ref, p_ref, pw_ref,
                      *, tile, fixed_shift, hp, qi, side_work):
    qT = qT_ref[0]
    qx = qx_ref[0]
    row = lax.broadcasted_iota(jnp.int32, qT.shape, 0)
    n_piece = 3 * N_HEADS
    for j in range(2):
        top = jnp.where((row // HEAD_DIM) == j, qT, jnp.zeros_like(qT))
        used = row < (2 * n_piece if fixed_shift else n_piece)
        bot = jnp.where(used & ((row % N_HEADS) == 2 * hp + j), qx, jnp.zeros_like(qx))
        qw_ref[:, j * tile:(j + 1) * tile] = jnp.concatenate([top, bot], axis=0)
    m_ref[...] = jnp.full_like(m_ref, NEG_INF)
    acc_ref[...] = jnp.zeros_like(acc_ref)
    ones = jnp.ones((acc_ref.shape[1] - HEAD_DIM, tile), bf16)

    def key_start(kt):
        return kt * tile if isinstance(kt, int) else pl.multiple_of(kt * tile, tile)

    def scores(kt, diagonal):
        k0 = key_start(kt)
        lhs = jnp.concatenate([kb_ref[0, pl.ds(k0, tile), :], ek_ref[0, pl.ds(k0, tile), :]], axis=1)
        s = _dot(lhs, qw_ref[...])
        if diagonal:
            kidx = lax.broadcasted_iota(jnp.int32, s.shape, 0)
            qidx = lax.broadcasted_iota(jnp.int32, s.shape, 1) % tile
            s = jnp.where(kidx <= qidx, s, NEG_INF)
        return s

    def values(kt, j):
        k0 = key_start(kt)
        return jnp.concatenate([vT_ref[0, j * HEAD_DIM:(j + 1) * HEAD_DIM, pl.ds(k0, tile)], ones], axis=0)

    if fixed_shift:
        def stage(kt, diagonal=False, first=False):
            if not first:
                pw_ref[...] = p_ref[...]
            p_ref[...] = jnp.exp2(scores(kt, diagonal)).astype(bf16)
            if not first:
                for j in range(2):
                    acc_ref[j] += _dot(values(kt - 1, j), pw_ref[:, j * tile:(j + 1) * tile])

        def stages(kt, n_tiles):
            for u in range(0, n_tiles, 2):
                pw_ref[...] = jnp.exp2(scores(kt + u, False)).astype(bf16)
                for j in range(2):
                    acc_ref[j] += _dot(values(kt + u - 1, j), p_ref[:, j * tile:(j + 1) * tile])
                p_ref[...] = jnp.exp2(scores(kt + u + 1, False)).astype(bf16)
                for j in range(2):
                    acc_ref[j] += _dot(values(kt + u, j), pw_ref[:, j * tile:(j + 1) * tile])

        @pl.when(qi > 0)
        def _():
            stage(0, first=True)
            n_trips = (qi - 1) // TILES_PER_TRIP
            lax.fori_loop(0, n_trips, lambda i, c: (stages(1 + TILES_PER_TRIP * i, TILES_PER_TRIP), c)[1], 0)
            done = 1 + TILES_PER_TRIP * n_trips

            @pl.when(qi - done >= 2)
            def _():
                stages(done, 2)

            @pl.when((qi - done) % 2 == 1)
            def _():
                stage(qi - 1)

        @pl.when(qi == 0)
        def _():
            p_ref[...] = jnp.zeros_like(p_ref)

        side_work[0]()
        pw_ref[...] = p_ref[...]
        p_ref[...] = jnp.exp2(scores(qi, True)).astype(bf16)
        side_work[1]()
        for j in range(2):
            acc_ref[j] += _dot(values(jnp.maximum(qi - 1, 0), j), pw_ref[:, j * tile:(j + 1) * tile])
        for j in range(2):
            acc_ref[j] += _dot(values(qi, j), p_ref[:, j * tile:(j + 1) * tile])
    else:
        for work in side_work:
            work()
        def step(kt, diagonal):
            s2 = scores(kt, diagonal)
            for j in range(2):
                s = s2[:, j * tile:(j + 1) * tile]
                m_old = m_ref[j]
                m_new = jnp.maximum(m_old, jnp.max(s, axis=0, keepdims=True))
                pv = _dot(values(kt, j), jnp.exp2(s - m_new).astype(bf16))
                acc_ref[j] = jnp.exp2(m_old - m_new) * acc_ref[j] + pv
                m_ref[j] = m_new

        lax.fori_loop(0, qi, lambda kt, c: (step(kt, False), c)[1], 0)
        step(qi, True)
    o = jnp.concatenate([acc_ref[j, 0:HEAD_DIM, :] / acc_ref[j, HEAD_DIM:HEAD_DIM + 1, :] for j in range(2)], axis=0)
    g_ref[0] = (o.T * sz_ref[0].astype(f32)).astype(bf16)


def _logf_suffix_kernel(lp_ref, t_ref, o_ref):
    o_ref[...] = _dot3_lhs(lp_ref[...], t_ref[...])


def _attn_sample_phases(q_ref, kn_ref, vn_ref, lfT_ref, sz_ref, ut_ref, k_refs, v_refs, e_refs, g_ref,
                        qbd_ref, kb_ref, vb_ref, run_ref, m_ref, l_ref, acc_ref, *, dec_seq):
    n_pg = len(k_refs)
    n_rows = dec_seq * N_HEADS
    carried = {}

    def online_update(s, v_bf, v_feature_major):
        m_old = m_ref[...]
        m_new = jnp.maximum(m_old, jnp.max(s, axis=1, keepdims=True))
        alpha = jnp.exp2(m_old - m_new)
        p = jnp.exp2(s - m_new).astype(bf16)
        l_ref[...] = alpha * l_ref[...] + jnp.sum(p.astype(f32), axis=1, keepdims=True)
        pv = _dot_nt(p, v_bf) if v_feature_major else _dot(p, v_bf)
        acc_ref[...] = alpha * acc_ref[...] + pv
        m_ref[...] = m_new

    def start():
        q = q_ref[...]
        q3 = jnp.broadcast_to(q[:, None, :], (dec_seq, N_HEADS, D_ATTN))
        hh = lax.broadcasted_iota(jnp.int32, q3.shape, 1)
        ln = lax.broadcasted_iota(jnp.int32, q3.shape, 2)
        qbd = jnp.where((ln // HEAD_DIM) == hh, q3, 0.0).reshape(n_rows, D_ATTN).astype(bf16)
        qbd_ref[...] = qbd
        m_ref[...] = jnp.full_like(m_ref, NEG_INF)
        l_ref[...] = jnp.zeros_like(l_ref)
        acc_ref[...] = jnp.zeros_like(acc_ref)
        run_ref[...] = jnp.zeros_like(run_ref)
        pad = jnp.zeros((LANES - dec_seq, D_ATTN), f32)
        kpad = jnp.concatenate([kn_ref[...], pad], axis=0).astype(bf16)
        vpad = jnp.concatenate([vn_ref[...], pad], axis=0).astype(bf16)
        s = _dot_nt(qbd, kpad)
        cnew = _dot3_lhs(lfT_ref[0], ut_ref[...])
        s3 = s.reshape(dec_seq, N_HEADS, LANES) - (cnew * LOG2E)[None]
        tt = lax.broadcasted_iota(jnp.int32, s3.shape, 0)
        jj = lax.broadcasted_iota(jnp.int32, s3.shape, 2)
        s3 = jnp.where(jj <= tt, s3, NEG_INF)
        online_update(s3.reshape(n_rows, LANES), vpad, False)

    def scores():
        bias = [None] * n_pg
        run = run_ref[...]
        for i in reversed(range(n_pg)):
            e = e_refs[i][...]
            bias[i] = (e[:, 0:LANES] + run) * LOG2E
            run = run + e[:, LANES:2 * LANES]
            kb_ref[:, i * LANES:(i + 1) * LANES] = k_refs[i][...].astype(bf16)
        run_ref[...] = run
        s = _dot(qbd_ref[...], kb_ref[...])
        b = jnp.concatenate(bias, axis=1)
        carried["s"] = (s.reshape(dec_seq, N_HEADS, n_pg * LANES) + b[None]).reshape(n_rows, n_pg * LANES)

    def values():
        for i in range(n_pg):
            vb_ref[:, i * LANES:(i + 1) * LANES] = v_refs[i][...].astype(bf16)
        online_update(carried.pop("s"), vb_ref[...], True)

    def finish():
        o = acc_ref[...] / l_ref[...]
        o3 = o.reshape(dec_seq, N_HEADS, D_ATTN)
        hh = lax.broadcasted_iota(jnp.int32, o3.shape, 1)
        ln = lax.broadcasted_iota(jnp.int32, o3.shape, 2)
        o2 = jnp.sum(jnp.where((ln // HEAD_DIM) == hh, o3, 0.0), axis=1)
        g_ref[...] = o2 * sz_ref[...]

    return start, scores, values, finish


N_PROMPT_IN = 6
N_PROMPT_SCRATCH = 5
N_SAMPLE_IN = 6
N_SAMPLE_SCRATCH = 7


def _attn_kernel(pt_ref, *refs, tile, fixed_shift, dec_seq, n_pg, n_chunks, n_sample_steps, n_prompt_steps):
    refs = list(refs)
    take = lambda n: [refs.pop(0) for _ in range(n)]
    prompt_in, sample_in = take(N_PROMPT_IN), take(N_SAMPLE_IN)
    cache_refs = take(3)
    g_ref, gs_ref = take(2)
    prompt_scratch, sample_scratch = take(N_PROMPT_SCRATCH), take(N_SAMPLE_SCRATCH)
    page_bufs = take(3)
    (sem,) = take(1)
    assert not refs
    b, hp, qi = pl.program_id(0), pl.program_id(1), pl.program_id(2)
    step = (b * pl.num_programs(1) + hp) * pl.num_programs(2) + qi

    def page_copies(m, slot):
        seq = m // n_chunks
        first_page = (n_chunks - 1 - m % n_chunks) * n_pg
        copies = []
        for i in range(n_pg):
            pg = pt_ref[seq, first_page + i]
            for a, (src, dst) in enumerate(zip(cache_refs, page_bufs)):
                copies.append(pltpu.make_async_copy(src.at[pg], dst.at[slot, i], sem.at[slot, a]))
        return copies

    @pl.when(step == 0)
    def _():
        for cp in page_copies(0, 0):
            cp.start()

    @pl.when(step + 1 < n_sample_steps)
    def _():
        for cp in page_copies(step + 1, (step + 1) % 2):
            cp.start()

    slot = step % 2
    chunk = step % n_chunks
    pages = [[buf.at[slot, i] for i in range(n_pg)] for buf in page_bufs]
    start, scores, values, finish = _attn_sample_phases(*sample_in, *pages, gs_ref, *sample_scratch, dec_seq=dec_seq)

    def decode_prologue():
        for cp in page_copies(step, slot):
            cp.wait()
        pl.when(chunk == 0)(start)

    if n_sample_steps == n_prompt_steps:
        decode_prologue()
        side_work = (scores, values)
    else:
        @pl.when(step < n_sample_steps)
        def _():
            decode_prologue()
            scores()
            values()
            pl.when(chunk == n_chunks - 1)(finish)
        side_work = (lambda: None, lambda: None)

    _attn_prompt_step(*prompt_in, g_ref, *prompt_scratch, tile=tile, fixed_shift=fixed_shift, hp=hp, qi=qi,
                      side_work=side_work)
    if n_sample_steps == n_prompt_steps:
        pl.when(chunk == n_chunks - 1)(finish)


def _outproj_kernel(x_ref, g_ref, mc_ref, sga_ref, wao_ref, wo_ref, y_ref):
    ya = _dot(g_ref[...].astype(bf16), wao_ref[...])
    merged = mc_ref[...].astype(f32) + sga_ref[...].astype(f32) * ya
    y_ref[...] = x_ref[...] + _dot(merged.astype(bf16), wo_ref[...])


def _row_tile(n, pref):
    t = min(n, pref)
    while n % t:
        t //= 2
    return t


def _layer(xp, xs, cache_k, cache_v, cache_logf, state_conv, page_table,
           norm_g, w_in, conv_w, q_norm_g, k_norm_g, f_bias, w_conv_out, w_attn_out, w_o):
    B, S, D = xp.shape
    NS, T, _ = xs.shape
    n_phys = cache_k.shape[0]
    n_pages = page_table.shape[1]
    dc = conv_w.shape[1]
    tm = _row_tile(S, 512)
    tile = _row_tile(S, 512)
    n_pg = min(PAGES_PER_STEP, n_pages)
    cp = functools.partial(pltpu.CompilerParams, vmem_limit_bytes=VMEM_LIMIT)

    o0 = 4 * dc
    wb = w_in.astype(bf16)
    w_c = wb[:, 0:o0]
    w_q = wb[:, o0:o0 + D_ATTN]
    w_k = wb[:, o0 + D_ATTN:o0 + 2 * D_ATTN]
    w_v = wb[:, o0 + 2 * D_ATTN:o0 + 3 * D_ATTN]
    w_az = wb[:, o0 + 3 * D_ATTN:o0 + 4 * D_ATTN]
    o1 = o0 + 4 * D_ATTN
    w_f = jnp.pad(wb[:, o1:o1 + N_HEADS], ((0, 0), (0, LANES - N_HEADS)))
    w_gc = wb[:, o1 + N_HEADS:o1 + N_HEADS + D]
    w_ga = wb[:, o1 + N_HEADS + D:o1 + N_HEADS + 2 * D]
    ng = norm_g.reshape(1, D)
    kg = jnp.tile(k_norm_g, N_HEADS)
    qg = jnp.tile(q_norm_g, N_HEADS) * (HEAD_DIM ** -0.5 * LOG2E)
    fb = jnp.pad(f_bias, (0, LANES - N_HEADS))
    wco = w_conv_out.astype(bf16)
    wao = w_attn_out.astype(bf16)
    wo = w_o.astype(bf16)
    common_w = (ng, w_c, w_az, w_gc, w_ga, conv_w, wco)
    common_specs = [_const_spec(a.shape) for a in common_w]

    bound = SHIFT_MARGIN * HEAD_DIM * jnp.max(jnp.abs(qg)) * jnp.max(jnp.abs(kg))

    triu = jnp.asarray(np.triu(np.ones((tm, tm), np.float32)), bf16)
    w_fT = w_f.T[0:2 * SUBLANES]
    prompt_w = (w_q.T, w_k.T, w_v.T, w_fT, qg.reshape(D_ATTN, 1), kg.reshape(D_ATTN, 1),
                fb[0:2 * SUBLANES].reshape(2 * SUBLANES, 1), triu, jnp.full((1, LANES), bound, f32))
    nt = S // tm
    row_spec = lambda w: pl.BlockSpec((1, tm, w), lambda b, j: (b, j, 0))
    colT_spec = lambda r: pl.BlockSpec((1, r, tm), lambda b, j: (b, 0, j))
    kT_p, kb_p, vT_p, vTb_p, qT_p, qx_p, ek_p, lfT_p, sz_p, sga_p, mc_p, cn_p = pl.pallas_call(
        _inproj_prompt_kernel,
        grid=(B, nt),
        in_specs=[row_spec(D)] + common_specs + [_const_spec(a.shape) for a in prompt_w],
        out_specs=[colT_spec(D_ATTN), row_spec(D_ATTN), colT_spec(D_ATTN), colT_spec(D_ATTN), colT_spec(D_ATTN),
                   colT_spec(LANES), row_spec(LANES), colT_spec(N_HEADS), row_spec(D_ATTN), row_spec(D), row_spec(D),
                   pl.BlockSpec((1, CONV_WIDTH - 1, dc), lambda b, j: (b, 0, 0))],
        out_shape=[jax.ShapeDtypeStruct((B, D_ATTN, S), f32), jax.ShapeDtypeStruct((B, S, D_ATTN), bf16),
                   jax.ShapeDtypeStruct((B, D_ATTN, S), f32), jax.ShapeDtypeStruct((B, D_ATTN, S), bf16),
                   jax.ShapeDtypeStruct((B, D_ATTN, S), bf16), jax.ShapeDtypeStruct((B, LANES, S), bf16),
                   jax.ShapeDtypeStruct((B, S, LANES), bf16),
                   jax.ShapeDtypeStruct((B, N_HEADS, S), f32), jax.ShapeDtypeStruct((B, S, D_ATTN), bf16),
                   jax.ShapeDtypeStruct((B, S, D), bf16), jax.ShapeDtypeStruct((B, S, D), bf16),
                   jax.ShapeDtypeStruct((B, CONV_WIDTH - 1, dc), f32)],
        scratch_shapes=[pltpu.VMEM((SUBLANES, dc), f32), pltpu.VMEM((2 * SUBLANES, LANES), f32)],
        compiler_params=cp(dimension_semantics=("arbitrary", "arbitrary")),
        name="inproj_prompt",
    )(xp, *common_w, *prompt_w)

    R = NS * T
    ts = _row_tile(R, 512)
    zrow = jnp.zeros((NS, T - 1, dc), f32)
    s1 = jnp.concatenate([state_conv[:, 1:2], zrow], axis=1).reshape(R, dc)
    s2 = jnp.concatenate([state_conv, zrow[:, 1:]], axis=1).reshape(R, dc)
    head_of = np.arange(D_ATTN) // HEAD_DIM
    bd = jnp.asarray(head_of[:, None] == head_of[None, :], bf16)
    sample_w = (w_q, w_k, w_v, w_f, qg.reshape(1, D_ATTN), kg.reshape(1, D_ATTN), fb.reshape(1, LANES), bd)
    rs = lambda w: pl.BlockSpec((ts, w), lambda i: (i, 0))
    k_s, v_s, q_s, lf_s, sz_s, sga_s, mc_s, u_s = pl.pallas_call(
        functools.partial(_inproj_sample_kernel, dec_seq=T),
        grid=(R // ts,),
        in_specs=[rs(D)] + common_specs + [_const_spec(a.shape) for a in sample_w] + [rs(dc), rs(dc)],
        out_specs=[rs(D_ATTN), rs(D_ATTN), rs(D_ATTN), rs(N_HEADS), rs(D_ATTN), rs(D), rs(D), rs(dc)],
        out_shape=[jax.ShapeDtypeStruct((R, D_ATTN), f32), jax.ShapeDtypeStruct((R, D_ATTN), f32),
                   jax.ShapeDtypeStruct((R, D_ATTN), f32), jax.ShapeDtypeStruct((R, N_HEADS), f32),
                   jax.ShapeDtypeStruct((R, D_ATTN), f32), jax.ShapeDtypeStruct((R, D), bf16),
                   jax.ShapeDtypeStruct((R, D), bf16), jax.ShapeDtypeStruct((R, dc), f32)],
        compiler_params=cp(dimension_semantics=("arbitrary",)),
        name="inproj_sample",
    )(xs.reshape(R, D), *common_w, *sample_w, s1, s2)

    page = cache_logf.shape[1]
    assert page == LANES
    pos = np.arange(page)
    tmat = jnp.asarray(np.concatenate([pos[:, None] > pos[None, :], np.ones((page, page), bool)], axis=1), bf16)
    lp2 = jnp.transpose(cache_logf, (0, 2, 1)).reshape(n_phys * N_HEADS, page)
    tp = _row_tile(n_phys * N_HEADS, 2048)
    esuf = pl.pallas_call(
        _logf_suffix_kernel,
        grid=(n_phys * N_HEADS // tp,),
        in_specs=[pl.BlockSpec((tp, page), lambda i: (i, 0)), _const_spec(tmat.shape)],
        out_specs=pl.BlockSpec((tp, 2 * page), lambda i: (i, 0)),
        out_shape=jax.ShapeDtypeStruct((n_phys * N_HEADS, 2 * page), f32),
        compiler_params=cp(dimension_semantics=("arbitrary",)),
        name="logf_suffix",
    )(lp2, tmat).reshape(n_phys, N_HEADS, 2 * page)

    nq = S // tile
    pair = 2 * HEAD_DIM
    n_prompt_steps = B * (N_HEADS // 2) * nq
    while NS * (n_pages // n_pg) > n_prompt_steps and n_pg < n_pages:
        n_pg *= 2
    n_chunks = n_pages // n_pg
    n_sample_steps = NS * n_chunks
    if n_pages % n_pg or n_sample_steps > n_prompt_steps:
        raise NotImplementedError("decode steps must fit in the prompt attention grid")
    lfT = jnp.pad(jnp.swapaxes(lf_s.reshape(NS, T, N_HEADS), 1, 2), ((0, 0), (0, 0), (0, LANES - T)))
    ut = jnp.asarray(np.triu(np.ones((LANES, LANES), np.float32)), bf16)
    ck2 = jnp.transpose(cache_k, (0, 2, 3, 1)).reshape(n_phys, D_ATTN, page)
    cv2 = jnp.transpose(cache_v, (0, 2, 3, 1)).reshape(n_phys, D_ATTN, page)

    def decode_step(b, hp, qi):
        return jnp.minimum((b * (N_HEADS // 2) + hp) * nq + qi, n_sample_steps - 1)

    seq_spec = pl.BlockSpec((T, D_ATTN), lambda b, hp, qi, pt: (decode_step(b, hp, qi) // n_chunks, 0))
    prompt_specs = [pl.BlockSpec((1, pair, tile), lambda b, hp, qi, pt: (b, hp, qi)),
                    pl.BlockSpec((1, LANES, tile), lambda b, hp, qi, pt: (b, 0, qi)),
                    pl.BlockSpec((1, S, pair), lambda b, hp, qi, pt: (b, 0, hp)),
                    pl.BlockSpec((1, S, LANES), lambda b, hp, qi, pt: (b, 0, 0)),
                    pl.BlockSpec((1, pair, S), lambda b, hp, qi, pt: (b, hp, 0)),
                    pl.BlockSpec((1, tile, pair), lambda b, hp, qi, pt: (b, qi, hp))]
    sample_specs = [seq_spec, seq_spec, seq_spec,
                    pl.BlockSpec((1, N_HEADS, LANES), lambda b, hp, qi, pt: (decode_step(b, hp, qi) // n_chunks, 0, 0)),
                    seq_spec,
                    pl.BlockSpec((LANES, LANES), lambda b, hp, qi, pt: (0, 0))]
    prompt_scratch = [pltpu.VMEM((2 * LANES, 2 * tile), bf16), pltpu.VMEM((2, 1, tile), f32),
                      pltpu.VMEM((2, HEAD_DIM + 2 * SUBLANES, tile), f32),
                      pltpu.VMEM((tile, 2 * tile), bf16), pltpu.VMEM((tile, 2 * tile), bf16)]
    sample_scratch = [pltpu.VMEM((T * N_HEADS, D_ATTN), bf16),
                      pltpu.VMEM((D_ATTN, n_pg * page), bf16), pltpu.VMEM((D_ATTN, n_pg * page), bf16),
                      pltpu.VMEM((N_HEADS, LANES), f32), pltpu.VMEM((T * N_HEADS, 1), f32),
                      pltpu.VMEM((T * N_HEADS, 1), f32), pltpu.VMEM((T * N_HEADS, D_ATTN), f32)]
    page_scratch = [pltpu.VMEM((2, n_pg, D_ATTN, page), f32), pltpu.VMEM((2, n_pg, D_ATTN, page), f32),
                    pltpu.VMEM((2, n_pg, N_HEADS, 2 * page), f32), pltpu.SemaphoreType.DMA((2, 3))]
    assert (len(prompt_specs), len(sample_specs)) == (N_PROMPT_IN, N_SAMPLE_IN)
    assert (len(prompt_scratch), len(sample_scratch)) == (N_PROMPT_SCRATCH, N_SAMPLE_SCRATCH)

    def attention(fixed_shift, *ops):
        return pl.pallas_call(
            functools.partial(_attn_kernel, tile=tile, fixed_shift=fixed_shift, dec_seq=T, n_pg=n_pg,
                              n_chunks=n_chunks, n_sample_steps=n_sample_steps, n_prompt_steps=n_prompt_steps),
            grid_spec=pltpu.PrefetchScalarGridSpec(
                num_scalar_prefetch=1,
                grid=(B, N_HEADS // 2, nq),
                in_specs=prompt_specs + sample_specs + [pl.BlockSpec(memory_space=pl.ANY)] * 3,
                out_specs=[pl.BlockSpec((1, tile, pair), lambda b, hp, qi, pt: (b, qi, hp)), seq_spec],
                scratch_shapes=prompt_scratch + sample_scratch + page_scratch),
            out_shape=[jax.ShapeDtypeStruct((B, S, D_ATTN), bf16), jax.ShapeDtypeStruct((R, D_ATTN), f32)],
            compiler_params=cp(dimension_semantics=("arbitrary", "arbitrary", "arbitrary")),
            name="attn_fixed_shift" if fixed_shift else "attn_online_max",
        )(*ops)

    g_p, g_s = lax.cond(bound < MAX_FIXED_SHIFT_BOUND, functools.partial(attention, True),
                        functools.partial(attention, False),
                        page_table, qT_p, qx_p, kb_p, ek_p, vTb_p, sz_p, q_s, k_s, v_s, lfT, sz_s, ut, ck2, cv2, esuf)

    def outproj(x2, g2, mc2, sga2, name):
        n = x2.shape[0]
        t = _row_tile(n, 512)
        rs = lambda w: pl.BlockSpec((t, w), lambda i: (i, 0))
        return pl.pallas_call(
            _outproj_kernel,
            grid=(n // t,),
            in_specs=[rs(D), rs(D_ATTN), rs(D), rs(D), _const_spec(wao.shape), _const_spec(wo.shape)],
            out_specs=rs(D),
            out_shape=jax.ShapeDtypeStruct((n, D), f32),
            compiler_params=cp(dimension_semantics=("arbitrary",)),
            name=name,
        )(x2, g2, mc2, sga2, wao, wo)

    y_p = outproj(xp.reshape(B * S, D), g_p.reshape(B * S, D_ATTN), mc_p.reshape(B * S, D),
                  sga_p.reshape(B * S, D), "outproj_prompt").reshape(B, S, D)
    y_s = outproj(xs.reshape(R, D), g_s, mc_s, sga_s, "outproj_sample").reshape(NS, T, D)

    unT = lambda a: jnp.transpose(a.reshape(B, N_HEADS, HEAD_DIM, S), (0, 3, 1, 2))
    return (y_p, y_s, unT(kT_p), unT(vT_p), jnp.transpose(lfT_p, (0, 2, 1)), cn_p,
            k_s.reshape(NS, T, N_HEADS, HEAD_DIM), v_s.reshape(NS, T, N_HEADS, HEAD_DIM),
            lf_s.reshape(NS, T, N_HEADS), u_s.reshape(NS, T, dc)[:, T - (CONV_WIDTH - 1):])


def kernel(x_prompt, x_sample, cache_k, cache_v, cache_logf, state_conv, page_table, norm_g, w_in, conv_w,
           q_norm_g, k_norm_g, f_bias, w_conv_out, w_attn_out, w_o):
    depth = norm_g.shape[0]
    xp, xs = x_prompt, x_sample
    outs = []
    for l in range(depth):
        res = _layer(xp, xs, cache_k[l], cache_v[l], cache_logf[l], state_conv[l], page_table,
                     norm_g[l], w_in[l], conv_w[l], q_norm_g[l], k_norm_g[l], f_bias[l],
                     w_conv_out[l], w_attn_out[l], w_o[l])
        xp, xs = res[0], res[1]
        outs.append(res[2:])
    stacked = tuple(jnp.stack([o[i] for o in outs]) for i in range(8))
    return (xp, xs) + stacked
```

```python
import functools

import jax
import jax.numpy as jnp
import numpy as np
from jax import lax
from jax.experimental import pallas as pl
from jax.experimental.pallas import tpu as pltpu

N_HEADS = 8
HEAD_DIM = 64
D_ATTN = N_HEADS * HEAD_DIM
CONV_WIDTH = 3
RMS_EPS = 1e-6
NEG_INF = -1e30
LOG2E = 1.4426950408889634
LANES = 128
SUBLANES = 8
VMEM_LIMIT = 56 * 1024 * 1024
PAGES_PER_STEP = 16
TILES_PER_TRIP = 4
SHIFT_MARGIN = 1.02
MAX_FIXED_SHIFT_BOUND = 40.0

f32 = jnp.float32
bf16 = jnp.bfloat16


def _dot(a, b):
    return jnp.dot(a, b, preferred_element_type=f32)


def _dot_nt(a, b):
    return lax.dot_general(a, b, (((1,), (1,)), ((), ())), preferred_element_type=f32)


def _split3(a):
    hi = a.astype(bf16)
    r = a - hi.astype(f32)
    mid = r.astype(bf16)
    lo = (r - mid.astype(f32)).astype(bf16)
    return hi, mid, lo


def _dot3_lhs(a, b01):
    hi, mid, lo = _split3(a)
    return _dot(hi, b01) + _dot(mid, b01) + _dot(lo, b01)


def _log_sigmoid(z):
    return jnp.minimum(z, 0.0) - jnp.log1p(jnp.exp(-jnp.abs(z)))


def _silu(z):
    return z * jax.nn.sigmoid(z)


def _const_spec(shape):
    nd = len(shape)
    return pl.BlockSpec(shape, lambda *_: (0,) * nd, pipeline_mode=pl.Buffered(1))


def _inproj_common(x, ng_ref, wc_ref, waz_ref, wgc_ref, wga_ref):
    ms = jnp.mean(x * x, axis=-1, keepdims=True)
    h = (x * lax.rsqrt(ms + RMS_EPS) * ng_ref[...]).astype(bf16)
    dc = wc_ref.shape[1] // 4
    cb = _dot(h, wc_ref[:, 0 * dc:1 * dc])
    cc = _dot(h, wc_ref[:, 1 * dc:2 * dc])
    cx = _dot(h, wc_ref[:, 2 * dc:3 * dc])
    cz = _dot(h, wc_ref[:, 3 * dc:4 * dc])
    u = cc * cx
    gate_c = cb * _silu(cz)
    sz = _silu(_dot(h, waz_ref[...])).astype(bf16)
    sgc = jax.nn.sigmoid(_dot(h, wgc_ref[...]))
    sga = jax.nn.sigmoid(_dot(h, wga_ref[...])).astype(bf16)
    return h, u, gate_c, sz, sgc, sga


def _head_norm_T(xT, g_col):
    n = xT.shape[1]
    x3 = xT.reshape(N_HEADS, HEAD_DIM, n)
    ms = jnp.mean(x3 * x3, axis=1, keepdims=True)
    return (x3 * lax.rsqrt(ms + RMS_EPS)).reshape(D_ATTN, n) * g_col


def _inproj_prompt_kernel(x_ref, ng_ref, wc_ref, waz_ref, wgc_ref, wga_ref, cw_ref, wco_ref,
                          wT_ref, qg_ref, kg_ref, fbc_ref, triu_ref, b2_ref,
                          kT_ref, kb_ref, vT_ref, vTb_ref, qT_ref, qx_ref, ek_ref, lfT_ref, sz_ref, sga_ref, mc_ref,
                          cn_ref, ucar_ref, ccar_ref):
    @pl.when(pl.program_id(1) == 0)
    def _():
        ucar_ref[...] = jnp.zeros_like(ucar_ref)
        ccar_ref[...] = jnp.zeros_like(ccar_ref)

    x = x_ref[0]
    tm = x.shape[0]
    ms = jnp.mean(x * x, axis=-1, keepdims=True)
    h = (x * lax.rsqrt(ms + RMS_EPS) * ng_ref[...]).astype(bf16)

    allT = _dot_nt(wT_ref[...], h)
    zT = allT[3 * D_ATTN:] + fbc_ref[...]
    hrow = lax.broadcasted_iota(jnp.int32, zT.shape, 0)
    lfT = jnp.where(hrow < N_HEADS, _log_sigmoid(zT), 0.0)
    lfT_ref[0] = lfT[0:N_HEADS, :]

    dc = wc_ref.shape[1] // 4
    cb = _dot(h, wc_ref[:, 0 * dc:1 * dc])
    cc = _dot(h, wc_ref[:, 1 * dc:2 * dc])
    cx = _dot(h, wc_ref[:, 2 * dc:3 * dc])
    cz = _dot(h, wc_ref[:, 3 * dc:4 * dc])

    cT = ccar_ref[:, 0:1] + _dot3_lhs(lfT, triu_ref[...])
    ccar_ref[...] = jnp.broadcast_to(cT[:, tm - 1:tm], ccar_ref.shape)

    knT = _head_norm_T(allT[D_ATTN:2 * D_ATTN], kg_ref[...])
    kT_ref[0] = knT
    kb_ref[0] = knT.T.astype(bf16)
    vT = allT[2 * D_ATTN:3 * D_ATTN]
    vT_ref[0] = vT
    vTb_ref[0] = vT.astype(bf16)
    qT_ref[0] = _head_norm_T(allT[0:D_ATTN], qg_ref[...]).astype(bf16)

    sz_ref[0] = _silu(_dot(h, waz_ref[...])).astype(bf16)
    sgc = jax.nn.sigmoid(_dot(h, wgc_ref[...]))
    sga_ref[0] = jax.nn.sigmoid(_dot(h, wga_ref[...])).astype(bf16)

    u = cc * cx
    rows = lax.broadcasted_iota(jnp.int32, u.shape, 0)
    prev1 = ucar_ref[SUBLANES - 1:SUBLANES, :]
    prev2 = ucar_ref[SUBLANES - 2:SUBLANES - 1, :]
    u1 = jnp.where(rows == 0, prev1, pltpu.roll(u, 1, 0))
    u2 = jnp.where(rows == 0, prev2, jnp.where(rows == 1, prev1, pltpu.roll(u, 2, 0)))
    cy = cw_ref[0:1, :] * u2 + cw_ref[1:2, :] * u1 + cw_ref[2:3, :] * u
    ucar_ref[...] = u[tm - SUBLANES:tm, :]
    cn_ref[0] = ucar_ref[SUBLANES - 2:SUBLANES, :]
    yc = _dot((cb * _silu(cz) * cy).astype(bf16), wco_ref[...])
    mc_ref[0] = (sgc * yc).astype(bf16)

    n_piece = 3 * N_HEADS
    khi, kmid, klo = [p.astype(f32)[0:N_HEADS] for p in _split3(cT * (-LOG2E))]
    shi, smid, slo = [p.astype(f32)[0:N_HEADS] for p in _split3(cT * LOG2E - b2_ref[0:1, 0:1])]
    ones = jnp.ones((n_piece, tm), f32)
    zeros = jnp.zeros((LANES - 2 * n_piece, tm), f32)
    ekT = jnp.concatenate([khi, kmid, klo, ones, zeros], axis=0)
    ek_ref[0] = ekT.T.astype(bf16)
    qx_ref[0] = jnp.concatenate([ones, shi, smid, slo, zeros], axis=0).astype(bf16)


def _head_norm(xf, g_row, bd):
    ssq = _dot((xf * xf).astype(bf16), bd)
    return xf * lax.rsqrt(ssq * (1.0 / HEAD_DIM) + RMS_EPS) * g_row


def _inproj_sample_kernel(x_ref, ng_ref, wc_ref, waz_ref, wgc_ref, wga_ref, cw_ref, wco_ref,
                          wq_ref, wk_ref, wv_ref, wf_ref, qg_ref, kg_ref, fb_ref, bd_ref, s1_ref, s2_ref,
                          k_ref, v_ref, q_ref, lf_ref, sz_ref, sga_ref, mc_ref, u_ref, *, dec_seq):
    x = x_ref[...]
    h, u, gate_c, sz, sgc, sga = _inproj_common(x, ng_ref, wc_ref, waz_ref, wgc_ref, wga_ref)
    t = lax.broadcasted_iota(jnp.int32, u.shape, 0) % dec_seq
    u1 = jnp.where(t == 0, s1_ref[...], pltpu.roll(u, 1, 0))
    u2 = jnp.where(t < 2, s2_ref[...], pltpu.roll(u, 2, 0))
    cy = cw_ref[0:1, :] * u2 + cw_ref[1:2, :] * u1 + cw_ref[2:3, :] * u
    u_ref[...] = u
    yc = _dot((gate_c * cy).astype(bf16), wco_ref[...])
    mc_ref[...] = (sgc * yc).astype(bf16)
    sga_ref[...] = sga
    sz_ref[...] = sz.astype(f32)
    k_ref[...] = _head_norm(_dot(h, wk_ref[...]), kg_ref[...], bd_ref[...])
    v_ref[...] = _dot(h, wv_ref[...])
    q_ref[...] = _head_norm(_dot(h, wq_ref[...]), qg_ref[...], bd_ref[...])
    lf_ref[...] = _log_sigmoid(_dot(h, wf_ref[...]) + fb_ref[...])[:, :N_HEADS]


def _attn_prompt_step(qT_ref, qx_ref, kb_ref, ek_ref, vT_ref, sz_ref, g_ref, qw_ref, m_ref, acc_ref, p_ref, pw_ref,
                      *, tile, fixed_shift, hp, qi, side_work):
    qT = qT_ref[0]
    qx = qx_ref[0]
    row = lax.broadcasted_iota(jnp.int32, qT.shape, 0)
    n_piece = 3 * N_HEADS
    for j in range(2):
        top = jnp.where((row // HEAD_DIM) == j, qT, jnp.zeros_like(qT))
        used = row < (2 * n_piece if fixed_shift else n_piece)
        bot = jnp.where(used & ((row % N_HEADS) == 2 * hp + j), qx, jnp.zeros_like(qx))
        qw_ref[:, j * tile:(j + 1) * tile] = jnp.concatenate([top, bot], axis=0)
    m_ref[...] = jnp.full_like(m_ref, NEG_INF)
    acc_ref[...] = jnp.zeros_like(acc_ref)
    ones = jnp.ones((acc_ref.shape[1] - HEAD_DIM, tile), bf16)

    def key_start(kt):
        return kt * tile if isinstance(kt, int) else pl.multiple_of(kt * tile, tile)

    def scores(kt, diagonal):
        k0 = key_start(kt)
        lhs = jnp.concatenate([kb_ref[0, pl.ds(k0, tile), :], ek_ref[0, pl.ds(k0, tile), :]], axis=1)
        s = _dot(lhs, qw_ref[...])
        if diagonal:
            kidx = lax.broadcasted_iota(jnp.int32, s.shape, 0)
            qidx = lax.broadcasted_iota(jnp.int32, s.shape, 1) % tile
            s = jnp.where(kidx <= qidx, s, NEG_INF)
        return s

    def values(kt, j):
        k0 = key_start(kt)
        return jnp.concatenate([vT_ref[0, j * HEAD_DIM:(j + 1) * HEAD_DIM, pl.ds(k0, tile)], ones], axis=0)

    if fixed_shift:
        def probs(kt, diagonal):
            return jnp.exp2(scores(kt, diagonal)).astype(bf16)

        def stage(kt, diagonal=False, first=False):
            if not first:
                pw_ref[...] = p_ref[...]
            p_ref[...] = probs(kt, diagonal)
            if not first:
                for j in range(2):
                    acc_ref[j] += _dot(values(kt - 1, j), pw_ref[:, j * tile:(j + 1) * tile])

        def stages(kt, n_tiles):
            for u in range(0, n_tiles, 2):
                pw_ref[...] = probs(kt + u, False)
                for j in range(2):
                    acc_ref[j] += _dot(values(kt + u - 1, j), p_ref[:, j * tile:(j + 1) * tile])
                p_ref[...] = probs(kt + u + 1, False)
                for j in range(2):
                    acc_ref[j] += _dot(values(kt + u, j), pw_ref[:, j * tile:(j + 1) * tile])

        @pl.when(qi > 0)
        def _():
            stage(0, first=True)
            n_trips = (qi - 1) // TILES_PER_TRIP
            lax.fori_loop(0, n_trips, lambda i, c: (stages(1 + TILES_PER_TRIP * i, TILES_PER_TRIP), c)[1], 0)
            done = 1 + TILES_PER_TRIP * n_trips

            @pl.when(qi - done >= 2)
            def _():
                stages(done, 2)

            @pl.when((qi - done) % 2 == 1)
            def _():
                stage(qi - 1)

        @pl.when(qi == 0)
        def _():
            p_ref[...] = jnp.zeros_like(p_ref)

        side_work[0]()
        pw_ref[...] = p_ref[...]
        p_ref[...] = probs(qi, True)
        side_work[1]()
        for j in range(2):
            acc_ref[j] += _dot(values(jnp.maximum(qi - 1, 0), j), pw_ref[:, j * tile:(j + 1) * tile])
        for j in range(2):
            acc_ref[j] += _dot(values(qi, j), p_ref[:, j * tile:(j + 1) * tile])
    else:
        for work in side_work:
            work()
        def step(kt, diagonal):
            s2 = scores(kt, diagonal)
            for j in range(2):
                s = s2[:, j * tile:(j + 1) * tile]
                m_old = m_ref[j]
                m_new = jnp.maximum(m_old, jnp.max(s, axis=0, keepdims=True))
                pv = _dot(values(kt, j), jnp.exp2(s - m_new).astype(bf16))
                acc_ref[j] = jnp.exp2(m_old - m_new) * acc_ref[j] + pv
                m_ref[j] = m_new

        lax.fori_loop(0, qi, lambda kt, c: (step(kt, False), c)[1], 0)
        step(qi, True)
    o = jnp.concatenate([acc_ref[j, 0:HEAD_DIM, :] / acc_ref[j, HEAD_DIM:HEAD_DIM + 1, :] for j in range(2)], axis=0)
    g_ref[0] = (o.T * sz_ref[0].astype(f32)).astype(bf16)


def _logf_suffix_kernel(lp_ref, t_ref, o_ref):
    o_ref[...] = _dot3_lhs(lp_ref[...], t_ref[...])


def _attn_sample_phases(q_ref, kn_ref, vn_ref, lfT_ref, sz_ref, ut_ref, k_refs, v_refs, e_refs, g_ref,
                        qbd_ref, kb_ref, vb_ref, run_ref, m_ref, l_ref, acc_ref, *, dec_seq):
    n_pg = len(k_refs)
    n_rows = dec_seq * N_HEADS
    carried = {}

    def online_update(s, v_bf, v_feature_major):
        m_old = m_ref[...]
        m_new = jnp.maximum(m_old, jnp.max(s, axis=1, keepdims=True))
        alpha = jnp.exp2(m_old - m_new)
        p = jnp.exp2(s - m_new).astype(bf16)
        l_ref[...] = alpha * l_ref[...] + jnp.sum(p.astype(f32), axis=1, keepdims=True)
        pv = _dot_nt(p, v_bf) if v_feature_major else _dot(p, v_bf)
        acc_ref[...] = alpha * acc_ref[...] + pv
        m_ref[...] = m_new

    def start():
        q = q_ref[...]
        q3 = jnp.broadcast_to(q[:, None, :], (dec_seq, N_HEADS, D_ATTN))
        hh = lax.broadcasted_iota(jnp.int32, q3.shape, 1)
        ln = lax.broadcasted_iota(jnp.int32, q3.shape, 2)
        qbd = jnp.where((ln // HEAD_DIM) == hh, q3, 0.0).reshape(n_rows, D_ATTN).astype(bf16)
        qbd_ref[...] = qbd
        m_ref[...] = jnp.full_like(m_ref, NEG_INF)
        l_ref[...] = jnp.zeros_like(l_ref)
        acc_ref[...] = jnp.zeros_like(acc_ref)
        run_ref[...] = jnp.zeros_like(run_ref)
        pad = jnp.zeros((LANES - dec_seq, D_ATTN), f32)
        kpad = jnp.concatenate([kn_ref[...], pad], axis=0).astype(bf16)
        vpad = jnp.concatenate([vn_ref[...], pad], axis=0).astype(bf16)
        s = _dot_nt(qbd, kpad)
        cnew = _dot3_lhs(lfT_ref[0], ut_ref[...])
        s3 = s.reshape(dec_seq, N_HEADS, LANES) - (cnew * LOG2E)[None]
        tt = lax.broadcasted_iota(jnp.int32, s3.shape, 0)
        jj = lax.broadcasted_iota(jnp.int32, s3.shape, 2)
        s3 = jnp.where(jj <= tt, s3, NEG_INF)
        online_update(s3.reshape(n_rows, LANES), vpad, False)

    def scores():
        bias = [None] * n_pg
        run = run_ref[...]
        for i in reversed(range(n_pg)):
            e = e_refs[i][...]
            bias[i] = (e[:, 0:LANES] + run) * LOG2E
            run = run + e[:, LANES:2 * LANES]
            kb_ref[:, i * LANES:(i + 1) * LANES] = k_refs[i][...].astype(bf16)
        run_ref[...] = run
        s = _dot(qbd_ref[...], kb_ref[...])
        b = jnp.concatenate(bias, axis=1)
        carried["s"] = (s.reshape(dec_seq, N_HEADS, n_pg * LANES) + b[None]).reshape(n_rows, n_pg * LANES)

    def values():
        for i in range(n_pg):
            vb_ref[:, i * LANES:(i + 1) * LANES] = v_refs[i][...].astype(bf16)
        online_update(carried.pop("s"), vb_ref[...], True)

    def finish():
        o = acc_ref[...] / l_ref[...]
        o3 = o.reshape(dec_seq, N_HEADS, D_ATTN)
        hh = lax.broadcasted_iota(jnp.int32, o3.shape, 1)
        ln = lax.broadcasted_iota(jnp.int32, o3.shape, 2)
        o2 = jnp.sum(jnp.where((ln // HEAD_DIM) == hh, o3, 0.0), axis=1)
        g_ref[...] = o2 * sz_ref[...]

    return start, scores, values, finish


N_PROMPT_IN = 6
N_PROMPT_SCRATCH = 5
N_SAMPLE_IN = 6
N_SAMPLE_SCRATCH = 7


def _attn_kernel(pt_ref, *refs, tile, fixed_shift, dec_seq, n_pg, n_chunks, n_sample_steps, n_prompt_steps):
    refs = list(refs)
    take = lambda n: [refs.pop(0) for _ in range(n)]
    prompt_in, sample_in = take(N_PROMPT_IN), take(N_SAMPLE_IN)
    cache_refs = take(3)
    g_ref, gs_ref = take(2)
    prompt_scratch, sample_scratch = take(N_PROMPT_SCRATCH), take(N_SAMPLE_SCRATCH)
    page_bufs = take(3)
    (sem,) = take(1)
    assert not refs
    b, hp, qi = pl.program_id(0), pl.program_id(1), pl.program_id(2)
    step = (b * pl.num_programs(1) + hp) * pl.num_programs(2) + qi

    def page_copies(m, slot, for_wait=False):
        seq = m // n_chunks
        first_page = (n_chunks - 1 - m % n_chunks) * n_pg
        copies = []
        for i in range(n_pg):
            pg = 0 if for_wait else pt_ref[seq, first_page + i]
            for a, (src, dst) in enumerate(zip(cache_refs, page_bufs)):
                copies.append(pltpu.make_async_copy(src.at[pg], dst.at[slot, i], sem.at[slot, a]))
        return copies

    @pl.when(step == 0)
    def _():
        for cp in page_copies(0, 0):
            cp.start()

    @pl.when(step + 1 < n_sample_steps)
    def _():
        for cp in page_copies(step + 1, (step + 1) % 2):
            cp.start()

    slot = step % 2
    chunk = step % n_chunks
    pages = [[buf.at[slot, i] for i in range(n_pg)] for buf in page_bufs]
    start, scores, values, finish = _attn_sample_phases(*sample_in, *pages, gs_ref, *sample_scratch, dec_seq=dec_seq)

    def decode_prologue():
        for cp in page_copies(step, slot, for_wait=True):
            cp.wait()
        pl.when(chunk == 0)(start)

    if n_sample_steps == n_prompt_steps:
        decode_prologue()
        side_work = (scores, values)
    else:
        @pl.when(step < n_sample_steps)
        def _():
            decode_prologue()
            scores()
            values()
            pl.when(chunk == n_chunks - 1)(finish)
        side_work = (lambda: None, lambda: None)

    _attn_prompt_step(*prompt_in, g_ref, *prompt_scratch, tile=tile, fixed_shift=fixed_shift, hp=hp, qi=qi,
                      side_work=side_work)
    if n_sample_steps == n_prompt_steps:
        pl.when(chunk == n_chunks - 1)(finish)


def _outproj_kernel(x_ref, g_ref, mc_ref, sga_ref, wao_ref, wo_ref, y_ref):
    ya = _dot(g_ref[...].astype(bf16), wao_ref[...])
    merged = mc_ref[...].astype(f32) + sga_ref[...].astype(f32) * ya
    y_ref[...] = x_ref[...] + _dot(merged.astype(bf16), wo_ref[...])


def _row_tile(n, pref):
    t = min(n, pref)
    while n % t:
        t //= 2
    return t


def _layer(xp, xs, cache_k, cache_v, cache_logf, state_conv, page_table,
           norm_g, w_in, conv_w, q_norm_g, k_norm_g, f_bias, w_conv_out, w_attn_out, w_o):
    B, S, D = xp.shape
    NS, T, _ = xs.shape
    n_phys = cache_k.shape[0]
    n_pages = page_table.shape[1]
    dc = conv_w.shape[1]
    tm = _row_tile(S, 512)
    tile = _row_tile(S, 512)
    n_pg = min(PAGES_PER_STEP, n_pages)
    cp = functools.partial(pltpu.CompilerParams, vmem_limit_bytes=VMEM_LIMIT)

    o0 = 4 * dc
    wb = w_in.astype(bf16)
    w_c = wb[:, 0:o0]
    w_q = wb[:, o0:o0 + D_ATTN]
    w_k = wb[:, o0 + D_ATTN:o0 + 2 * D_ATTN]
    w_v = wb[:, o0 + 2 * D_ATTN:o0 + 3 * D_ATTN]
    w_az = wb[:, o0 + 3 * D_ATTN:o0 + 4 * D_ATTN]
    o1 = o0 + 4 * D_ATTN
    w_f = jnp.pad(wb[:, o1:o1 + N_HEADS], ((0, 0), (0, LANES - N_HEADS)))
    w_gc = wb[:, o1 + N_HEADS:o1 + N_HEADS + D]
    w_ga = wb[:, o1 + N_HEADS + D:o1 + N_HEADS + 2 * D]
    ng = norm_g.reshape(1, D)
    kg = jnp.tile(k_norm_g, N_HEADS)
    qg = jnp.tile(q_norm_g, N_HEADS) * (HEAD_DIM ** -0.5 * LOG2E)
    fb = jnp.pad(f_bias, (0, LANES - N_HEADS))
    wco = w_conv_out.astype(bf16)
    wao = w_attn_out.astype(bf16)
    wo = w_o.astype(bf16)
    common_w = (ng, w_c, w_az, w_gc, w_ga, conv_w, wco)
    common_specs = [_const_spec(a.shape) for a in common_w]

    bound = SHIFT_MARGIN * HEAD_DIM * jnp.max(jnp.abs(qg)) * jnp.max(jnp.abs(kg))

    triu = jnp.asarray(np.triu(np.ones((tm, tm), np.float32)), bf16)
    w_fT = w_f.T[0:2 * SUBLANES]
    prompt_w = (jnp.concatenate([w_q.T, w_k.T, w_v.T, w_fT], axis=0), qg.reshape(D_ATTN, 1), kg.reshape(D_ATTN, 1),
                fb[0:2 * SUBLANES].reshape(2 * SUBLANES, 1), triu, jnp.full((1, LANES), bound, f32))
    nt = S // tm
    row_spec = lambda w: pl.BlockSpec((1, tm, w), lambda b, j: (b, j, 0))
    colT_spec = lambda r: pl.BlockSpec((1, r, tm), lambda b, j: (b, 0, j))
    kT_p, kb_p, vT_p, vTb_p, qT_p, qx_p, ek_p, lfT_p, sz_p, sga_p, mc_p, cn_p = pl.pallas_call(
        _inproj_prompt_kernel,
        grid=(B, nt),
        in_specs=[row_spec(D)] + common_specs + [_const_spec(a.shape) for a in prompt_w],
        out_specs=[colT_spec(D_ATTN), row_spec(D_ATTN), colT_spec(D_ATTN), colT_spec(D_ATTN), colT_spec(D_ATTN),
                   colT_spec(LANES), row_spec(LANES), colT_spec(N_HEADS), row_spec(D_ATTN), row_spec(D), row_spec(D),
                   pl.BlockSpec((1, CONV_WIDTH - 1, dc), lambda b, j: (b, 0, 0))],
        out_shape=[jax.ShapeDtypeStruct((B, D_ATTN, S), f32), jax.ShapeDtypeStruct((B, S, D_ATTN), bf16),
                   jax.ShapeDtypeStruct((B, D_ATTN, S), f32), jax.ShapeDtypeStruct((B, D_ATTN, S), bf16),
                   jax.ShapeDtypeStruct((B, D_ATTN, S), bf16), jax.ShapeDtypeStruct((B, LANES, S), bf16),
                   jax.ShapeDtypeStruct((B, S, LANES), bf16),
                   jax.ShapeDtypeStruct((B, N_HEADS, S), f32), jax.ShapeDtypeStruct((B, S, D_ATTN), bf16),
                   jax.ShapeDtypeStruct((B, S, D), bf16), jax.ShapeDtypeStruct((B, S, D), bf16),
                   jax.ShapeDtypeStruct((B, CONV_WIDTH - 1, dc), f32)],
        scratch_shapes=[pltpu.VMEM((SUBLANES, dc), f32), pltpu.VMEM((2 * SUBLANES, LANES), f32)],
        compiler_params=cp(dimension_semantics=("arbitrary", "arbitrary")),
        name="inproj_prompt",
    )(xp, *common_w, *prompt_w)

    R = NS * T
    ts = _row_tile(R, 512)
    zrow = jnp.zeros((NS, T - 1, dc), f32)
    s1 = jnp.concatenate([state_conv[:, 1:2], zrow], axis=1).reshape(R, dc)
    s2 = jnp.concatenate([state_conv, zrow[:, 1:]], axis=1).reshape(R, dc)
    head_of = np.arange(D_ATTN) // HEAD_DIM
    bd = jnp.asarray(head_of[:, None] == head_of[None, :], bf16)
    sample_w = (w_q, w_k, w_v, w_f, qg.reshape(1, D_ATTN), kg.reshape(1, D_ATTN), fb.reshape(1, LANES), bd)
    rs = lambda w: pl.BlockSpec((ts, w), lambda i: (i, 0))
    k_s, v_s, q_s, lf_s, sz_s, sga_s, mc_s, u_s = pl.pallas_call(
        functools.partial(_inproj_sample_kernel, dec_seq=T),
        grid=(R // ts,),
        in_specs=[rs(D)] + common_specs + [_const_spec(a.shape) for a in sample_w] + [rs(dc), rs(dc)],
        out_specs=[rs(D_ATTN), rs(D_ATTN), rs(D_ATTN), rs(N_HEADS), rs(D_ATTN), rs(D), rs(D), rs(dc)],
        out_shape=[jax.ShapeDtypeStruct((R, D_ATTN), f32), jax.ShapeDtypeStruct((R, D_ATTN), f32),
                   jax.ShapeDtypeStruct((R, D_ATTN), f32), jax.ShapeDtypeStruct((R, N_HEADS), f32),
                   jax.ShapeDtypeStruct((R, D_ATTN), f32), jax.ShapeDtypeStruct((R, D), bf16),
                   jax.ShapeDtypeStruct((R, D), bf16), jax.ShapeDtypeStruct((R, dc), f32)],
        compiler_params=cp(dimension_semantics=("arbitrary",)),
        name="inproj_sample",
    )(xs.reshape(R, D), *common_w, *sample_w, s1, s2)

    page = cache_logf.shape[1]
    assert page == LANES
    pos = np.arange(page)
    tmat = jnp.asarray(np.concatenate([pos[:, None] > pos[None, :], np.ones((page, page), bool)], axis=1), bf16)
    lp2 = jnp.transpose(cache_logf, (0, 2, 1)).reshape(n_phys * N_HEADS, page)
    tp = _row_tile(n_phys * N_HEADS, 2048)
    esuf = pl.pallas_call(
        _logf_suffix_kernel,
        grid=(n_phys * N_HEADS // tp,),
        in_specs=[pl.BlockSpec((tp, page), lambda i: (i, 0)), _const_spec(tmat.shape)],
        out_specs=pl.BlockSpec((tp, 2 * page), lambda i: (i, 0)),
        out_shape=jax.ShapeDtypeStruct((n_phys * N_HEADS, 2 * page), f32),
        compiler_params=cp(dimension_semantics=("arbitrary",)),
        name="logf_suffix",
    )(lp2, tmat).reshape(n_phys, N_HEADS, 2 * page)

    nq = S // tile
    pair = 2 * HEAD_DIM
    n_prompt_steps = B * (N_HEADS // 2) * nq
    while NS * (n_pages // n_pg) > n_prompt_steps and n_pg < n_pages:
        n_pg *= 2
    n_chunks = n_pages // n_pg
    n_sample_steps = NS * n_chunks
    if n_pages % n_pg or n_sample_steps > n_prompt_steps:
        raise NotImplementedError("decode steps must fit in the prompt attention grid")
    lfT = jnp.pad(jnp.swapaxes(lf_s.reshape(NS, T, N_HEADS), 1, 2), ((0, 0), (0, 0), (0, LANES - T)))
    ut = jnp.asarray(np.triu(np.ones((LANES, LANES), np.float32)), bf16)
    ck2 = jnp.transpose(cache_k, (0, 2, 3, 1)).reshape(n_phys, D_ATTN, page)
    cv2 = jnp.transpose(cache_v, (0, 2, 3, 1)).reshape(n_phys, D_ATTN, page)

    def decode_step(b, hp, qi):
        return jnp.minimum((b * (N_HEADS // 2) + hp) * nq + qi, n_sample_steps - 1)

    seq_spec = pl.BlockSpec((T, D_ATTN), lambda b, hp, qi, pt: (decode_step(b, hp, qi) // n_chunks, 0))
    prompt_specs = [pl.BlockSpec((1, pair, tile), lambda b, hp, qi, pt: (b, hp, qi)),
                    pl.BlockSpec((1, LANES, tile), lambda b, hp, qi, pt: (b, 0, qi)),
                    pl.BlockSpec((1, S, pair), lambda b, hp, qi, pt: (b, 0, hp)),
                    pl.BlockSpec((1, S, LANES), lambda b, hp, qi, pt: (b, 0, 0)),
                    pl.BlockSpec((1, pair, S), lambda b, hp, qi, pt: (b, hp, 0)),
                    pl.BlockSpec((1, tile, pair), lambda b, hp, qi, pt: (b, qi, hp))]
    sample_specs = [seq_spec, seq_spec, seq_spec,
                    pl.BlockSpec((1, N_HEADS, LANES), lambda b, hp, qi, pt: (decode_step(b, hp, qi) // n_chunks, 0, 0)),
                    seq_spec,
                    pl.BlockSpec((LANES, LANES), lambda b, hp, qi, pt: (0, 0))]
    prompt_scratch = [pltpu.VMEM((2 * LANES, 2 * tile), bf16), pltpu.VMEM((2, 1, tile), f32),
                      pltpu.VMEM((2, HEAD_DIM + 2 * SUBLANES, tile), f32),
                      pltpu.VMEM((tile, 2 * tile), bf16), pltpu.VMEM((tile, 2 * tile), bf16)]
    sample_scratch = [pltpu.VMEM((T * N_HEADS, D_ATTN), bf16),
                      pltpu.VMEM((D_ATTN, n_pg * page), bf16), pltpu.VMEM((D_ATTN, n_pg * page), bf16),
                      pltpu.VMEM((N_HEADS, LANES), f32), pltpu.VMEM((T * N_HEADS, 1), f32),
                      pltpu.VMEM((T * N_HEADS, 1), f32), pltpu.VMEM((T * N_HEADS, D_ATTN), f32)]
    page_scratch = [pltpu.VMEM((2, n_pg, D_ATTN, page), f32), pltpu.VMEM((2, n_pg, D_ATTN, page), f32),
                    pltpu.VMEM((2, n_pg, N_HEADS, 2 * page), f32), pltpu.SemaphoreType.DMA((2, 3))]
    assert (len(prompt_specs), len(sample_specs)) == (N_PROMPT_IN, N_SAMPLE_IN)
    assert (len(prompt_scratch), len(sample_scratch)) == (N_PROMPT_SCRATCH, N_SAMPLE_SCRATCH)

    def attention(fixed_shift, *ops):
        return pl.pallas_call(
            functools.partial(_attn_kernel, tile=tile, fixed_shift=fixed_shift, dec_seq=T, n_pg=n_pg,
                              n_chunks=n_chunks, n_sample_steps=n_sample_steps, n_prompt_steps=n_prompt_steps),
            grid_spec=pltpu.PrefetchScalarGridSpec(
                num_scalar_prefetch=1,
                grid=(B, N_HEADS // 2, nq),
                in_specs=prompt_specs + sample_specs + [pl.BlockSpec(memory_space=pl.ANY)] * 3,
                out_specs=[pl.BlockSpec((1, tile, pair), lambda b, hp, qi, pt: (b, qi, hp)), seq_spec],
                scratch_shapes=prompt_scratch + sample_scratch + page_scratch),
            out_shape=[jax.ShapeDtypeStruct((B, S, D_ATTN), bf16), jax.ShapeDtypeStruct((R, D_ATTN), f32)],
            compiler_params=cp(dimension_semantics=("arbitrary", "arbitrary", "arbitrary")),
            name="attn_fixed_shift" if fixed_shift else "attn_online_max",
        )(*ops)

    g_p, g_s = lax.cond(bound < MAX_FIXED_SHIFT_BOUND, functools.partial(attention, True),
                        functools.partial(attention, False),
                        page_table, qT_p, qx_p, kb_p, ek_p, vTb_p, sz_p, q_s, k_s, v_s, lfT, sz_s, ut, ck2, cv2, esuf)

    def outproj(x2, g2, mc2, sga2, name):
        n = x2.shape[0]
        t = _row_tile(n, 1024)
        rs = lambda w: pl.BlockSpec((t, w), lambda i: (i, 0))
        return pl.pallas_call(
            _outproj_kernel,
            grid=(n // t,),
            in_specs=[rs(D), rs(D_ATTN), rs(D), rs(D), _const_spec(wao.shape), _const_spec(wo.shape)],
            out_specs=rs(D),
            out_shape=jax.ShapeDtypeStruct((n, D), f32),
            compiler_params=cp(dimension_semantics=("arbitrary",)),
            name=name,
        )(x2, g2, mc2, sga2, wao, wo)

    y_p = outproj(xp.reshape(B * S, D), g_p.reshape(B * S, D_ATTN), mc_p.reshape(B * S, D),
                  sga_p.reshape(B * S, D), "outproj_prompt").reshape(B, S, D)
    y_s = outproj(xs.reshape(R, D), g_s, mc_s, sga_s, "outproj_sample").reshape(NS, T, D)

    unT = lambda a: jnp.transpose(a.reshape(B, N_HEADS, HEAD_DIM, S), (0, 3, 1, 2))
    return (y_p, y_s, unT(kT_p), unT(vT_p), jnp.transpose(lfT_p, (0, 2, 1)), cn_p,
            k_s.reshape(NS, T, N_HEADS, HEAD_DIM), v_s.reshape(NS, T, N_HEADS, HEAD_DIM),
            lf_s.reshape(NS, T, N_HEADS), u_s.reshape(NS, T, dc)[:, T - (CONV_WIDTH - 1):])


def kernel(x_prompt, x_sample, cache_k, cache_v, cache_logf, state_conv, page_table, norm_g, w_in, conv_w,
           q_norm_g, k_norm_g, f_bias, w_conv_out, w_attn_out, w_o):
    depth = norm_g.shape[0]
    xp, xs = x_prompt, x_sample
    outs = []
    for l in range(depth):
        res = _layer(xp, xs, cache_k[l], cache_v[l], cache_logf[l], state_conv[l], page_table,
                     norm_g[l], w_in[l], conv_w[l], q_norm_g[l], k_norm_g[l], f_bias[l],
                     w_conv_out[l], w_attn_out[l], w_o[l])
        xp, xs = res[0], res[1]
        outs.append(res[2:])
    stacked = tuple(jnp.stack([o[i] for o in outs]) for i in range(8))
    return (xp, xs) + stacked
```
